```python
import jax, jax.numpy as jnp
from jax import lax
import numpy as np

D_MODEL = 1024
BATCH = 8
SEQ = 2048
DEPTH = 2

GRID_W = 64
CTX_LEN = 256
D_LRU = D_MODEL // 2
D_CONV = D_MODEL - D_LRU
LRU_HEADS = 8
LRU_HEAD_DIM = D_LRU // LRU_HEADS
CONV_GROUPS = 8
D_IN = 2 * D_LRU + 3 * D_CONV
D_FF = 4 * D_MODEL
LRU_CONV_W = 4
SHORT_CONV_W = 3
RG_C = 8.0
N_MOD = 6
EPS = 1e-6

kernel_name = 'hybrid_rglru_shortconv_prefix_dit_block'


def _rmsnorm(x, g):
    xf = x.astype(jnp.float32)
    y = xf * lax.rsqrt(jnp.mean(xf * xf, axis=-1, keepdims=True) + EPS)
    return (y * g.astype(jnp.float32)).astype(x.dtype)


def _modulate(h, shift, scale):
    return h * (1 + scale) + shift


def _dwconv1d(v, w, pad_lo, pad_hi):
    return lax.conv_general_dilated(v, w[:, None, :].astype(v.dtype), window_strides=(1,),
                                    padding=[(pad_lo, pad_hi)],
                                    dimension_numbers=('NWC', 'WIO', 'NWC'),
                                    feature_group_count=v.shape[-1])


def _dwconv2d(v, w):
    kh, kw, ch = w.shape
    return lax.conv_general_dilated(v, w[:, :, None, :].astype(v.dtype), window_strides=(1, 1),
                                    padding=[((kh - 1) // 2, kh // 2), ((kw - 1) // 2, kw // 2)],
                                    dimension_numbers=('NHWC', 'HWIO', 'NHWC'),
                                    feature_group_count=ch)


def _combine(e1, e2):
    a1, b1 = e1
    a2, b2 = e2
    return a1 * a2, a2 * b1 + b2


def _linear_scan(a, b, h0, reverse):
    if h0 is not None:
        edge = -1 if reverse else 0
        b = b.at[:, edge].add(a[:, edge] * h0)
    _, h = lax.associative_scan(_combine, (a, b), reverse=reverse, axis=1)
    return h


def _rglru_coeffs(v, w_a, b_a, w_x, b_x, lam):
    vf = v.astype(jnp.float32)
    vh = vf.reshape(vf.shape[:-1] + (LRU_HEADS, LRU_HEAD_DIM))
    r = jax.nn.sigmoid(jnp.einsum('bshd,hde->bshe', vh, w_a.astype(jnp.float32)).reshape(vf.shape)
                       + b_a.astype(jnp.float32))
    i = jax.nn.sigmoid(jnp.einsum('bshd,hde->bshe', vh, w_x.astype(jnp.float32)).reshape(vf.shape)
                       + b_x.astype(jnp.float32))
    log_a = -RG_C * r * jax.nn.softplus(-lam.astype(jnp.float32))
    a = jnp.exp(log_a)
    b = jnp.sqrt(-jnp.expm1(2.0 * log_a)) * (i * vf)
    return a, b


def _token_mixers(h_lat, h_ctx, w_in, conv4_w, conv4_b, gate_a_w, gate_a_b, gate_x_w, gate_x_b,
                  rg_lambda, conv3_w, g_out_lru, g_out_conv, w_out, ctx_out):
    bsz, seq, _ = h_lat.shape
    rows = seq // GRID_W
    dt = h_lat.dtype
    u_lat = h_lat @ w_in
    u_ctx = h_ctx @ (w_in if ctx_out else w_in[:, :D_LRU])

    v_lat = _dwconv1d(u_lat[..., :D_LRU], conv4_w, 1, 2) + conv4_b
    v_ctx = _dwconv1d(u_ctx[..., :D_LRU], conv4_w, 1, 2) + conv4_b
    hs_lat, hs_ctx = [], []
    for d, rev in enumerate((False, True)):
        a_c, b_c = _rglru_coeffs(v_ctx, gate_a_w[d], gate_a_b[d], gate_x_w[d], gate_x_b[d], rg_lambda[d])
        hc = _linear_scan(a_c, b_c, None, rev)
        h0 = hc[:, 0] if rev else hc[:, -1]
        a_l, b_l = _rglru_coeffs(v_lat, gate_a_w[d], gate_a_b[d], gate_x_w[d], gate_x_b[d], rg_lambda[d])
        hs_lat.append(_linear_scan(a_l, b_l, h0, rev))
        if ctx_out:
            hs_ctx.append(hc)
    y_lru_lat = jax.nn.gelu(u_lat[..., D_LRU:2 * D_LRU]) * (hs_lat[0] + hs_lat[1]).astype(dt)

    o = 2 * D_LRU
    xc, bg, cg = u_lat[..., o:o + D_CONV], u_lat[..., o + D_CONV:o + 2 * D_CONV], u_lat[..., o + 2 * D_CONV:]
    half = D_CONV // 2
    v = (cg * xc).reshape(bsz, rows, GRID_W, D_CONV)
    conv_row = _dwconv2d(v[..., :half], conv3_w[None, :, :half])
    conv_col = _dwconv2d(v[..., half:], conv3_w[:, None, half:])
    y_conv_lat = bg * jnp.concatenate([conv_row, conv_col], axis=-1).reshape(bsz, seq, D_CONV)

    out_lat = jnp.concatenate([_rmsnorm(y_lru_lat, g_out_lru), _rmsnorm(y_conv_lat, g_out_conv)], axis=-1) @ w_out
    if not ctx_out:
        return out_lat, None

    y_lru_ctx = jax.nn.gelu(u_ctx[..., D_LRU:2 * D_LRU]) * (hs_ctx[0] + hs_ctx[1]).astype(dt)
    xc_c, bg_c, cg_c = u_ctx[..., o:o + D_CONV], u_ctx[..., o + D_CONV:o + 2 * D_CONV], u_ctx[..., o + 2 * D_CONV:]
    y_conv_ctx = bg_c * _dwconv1d(cg_c * xc_c, conv3_w, 1, 1)
    out_ctx = jnp.concatenate([_rmsnorm(y_lru_ctx, g_out_lru), _rmsnorm(y_conv_ctx, g_out_conv)], axis=-1) @ w_out
    return out_lat, out_ctx


def _sq_relu_mlp(h, w1, w2):
    return jnp.square(jax.nn.relu(h @ w1)) @ w2


def setup_inputs(seed: int = 0) -> dict:
    key = jax.random.key(seed)
    ks = jax.random.split(key, 24)
    f32 = jnp.float32

    def nrm(k, shape, scale):
        return jax.random.normal(k, shape, f32) * scale

    u = jax.random.uniform(ks[13], (DEPTH, 2, D_LRU), f32, 0.9, 0.999)
    a_base = u ** (1.0 / RG_C)
    return {
        'x': nrm(ks[0], (BATCH, SEQ, D_MODEL), 1.0),
        'c': nrm(ks[1], (BATCH, D_MODEL), 1.0),
        'ctx': nrm(ks[2], (BATCH, CTX_LEN, D_MODEL), 1.0),
        'c_ctx': nrm(ks[3], (D_MODEL,), 1.0),
        'ada_w': nrm(ks[4], (DEPTH, D_MODEL, N_MOD * D_MODEL), 0.5 * D_MODEL ** -0.5),
        'ada_b': nrm(ks[5], (DEPTH, N_MOD * D_MODEL), 0.02),
        'norm1_g': 1.0 + nrm(ks[6], (DEPTH, D_MODEL), 0.02),
        'norm2_g': 1.0 + nrm(ks[7], (DEPTH, D_MODEL), 0.02),
        'w_in': nrm(ks[8], (DEPTH, D_MODEL, D_IN), D_MODEL ** -0.5),
        'conv4_w': nrm(ks[9], (DEPTH, LRU_CONV_W, D_LRU), LRU_CONV_W ** -0.5),
        'conv4_b': nrm(ks[10], (DEPTH, D_LRU), 0.02),
        'gate_a_w': nrm(ks[11], (DEPTH, 2, LRU_HEADS, LRU_HEAD_DIM, LRU_HEAD_DIM), LRU_HEAD_DIM ** -0.5),
        'gate_a_b': nrm(ks[12], (DEPTH, 2, D_LRU), 0.02),
        'gate_x_w': nrm(ks[14], (DEPTH, 2, LRU_HEADS, LRU_HEAD_DIM, LRU_HEAD_DIM), LRU_HEAD_DIM ** -0.5),
        'gate_x_b': nrm(ks[15], (DEPTH, 2, D_LRU), 0.02),
        'rg_lambda': jnp.log(a_base) - jnp.log1p(-a_base),
        'conv3_w': nrm(ks[16], (DEPTH, SHORT_CONV_W, D_CONV), SHORT_CONV_W ** -0.5),
        'g_out_lru': 1.0 + nrm(ks[17], (DEPTH, D_LRU), 0.02),
        'g_out_conv': 1.0 + nrm(ks[18], (DEPTH, D_CONV), 0.02),
        'w_out': nrm(ks[19], (DEPTH, D_MODEL, D_MODEL), D_MODEL ** -0.5),
        'w_mlp1': nrm(ks[20], (DEPTH, D_MODEL, D_FF), D_MODEL ** -0.5),
        'w_mlp2': nrm(ks[21], (DEPTH, D_FF, D_MODEL), D_FF ** -0.5),
        'final_g': 1.0 + nrm(ks[22], (D_MODEL,), 0.02),
    }


def reference(x, c, ctx, c_ctx, ada_w, ada_b, norm1_g, norm2_g, w_in, conv4_w, conv4_b, gate_a_w, gate_a_b,
              gate_x_w, gate_x_b, rg_lambda, conv3_w, g_out_lru, g_out_conv, w_out, w_mlp1, w_mlp2, final_g):
    silu_c = jax.nn.silu(c)
    silu_cc = jax.nn.silu(c_ctx)
    for l in range(DEPTH):
        last = l == DEPTH - 1
        mod_lat = (silu_c @ ada_w[l] + ada_b[l])[:, None, :]
        mod_ctx = silu_cc @ ada_w[l] + ada_b[l]
        sh1, sc1, g1, sh2, sc2, g2 = jnp.split(mod_lat, N_MOD, axis=-1)
        sh1c, sc1c, g1c, sh2c, sc2c, g2c = jnp.split(mod_ctx, N_MOD, axis=-1)

        h_lat = _modulate(_rmsnorm(x, norm1_g[l]), sh1, sc1)
        h_ctx = _modulate(_rmsnorm(ctx, norm1_g[l]), sh1c, sc1c)
        mix_lat, mix_ctx = _token_mixers(h_lat, h_ctx, w_in[l], conv4_w[l], conv4_b[l], gate_a_w[l], gate_a_b[l],
                                         gate_x_w[l], gate_x_b[l], rg_lambda[l], conv3_w[l], g_out_lru[l],
                                         g_out_conv[l], w_out[l], not last)
        x = x + g1 * mix_lat
        x = x + g2 * _sq_relu_mlp(_modulate(_rmsnorm(x, norm2_g[l]), sh2, sc2), w_mlp1[l], w_mlp2[l])
        if not last:
            ctx = ctx + g1c * mix_ctx
            ctx = ctx + g2c * _sq_relu_mlp(_modulate(_rmsnorm(ctx, norm2_g[l]), sh2c, sc2c), w_mlp1[l], w_mlp2[l])
    return _rmsnorm(x, final_g)
```

```python
import functools

import jax
import jax.numpy as jnp
from jax import lax
from jax.experimental import pallas as pl
from jax.experimental.pallas import tpu as pltpu

F32 = jnp.float32
BF16 = jnp.bfloat16

D_MODEL = 1024
D_LRU = 512
D_CONV = 512
HALF_CONV = D_CONV // 2
D_FF = 4 * D_MODEL
GRID_W = 64
LRU_HEADS = 8
LRU_HEAD_DIM = D_LRU // LRU_HEADS
N_MOD = 6
N_PIECES = 5
RG_C = 8.0
EPS = 1e-6

SUBLANES = 8
GATE_BLOCK = 256
MOD_ROWS = 16
VMEM_LIMIT_BYTES = 56 * 1024 * 1024

ADA_COLS = 1536
PROJ_ROWS = 512
POST_ROWS = 512
FF_CHUNK = 1024
COEFF_ROWS = 256
SCAN_UNROLL = 4


def _sigmoid(z):
    return 0.5 * (jnp.tanh(0.5 * z) + 1.0)


def _gelu_tanh(z):
    return 0.5 * z * (1.0 + jnp.tanh(0.7978845608028654 * (z + 0.044715 * (z * z * z))))


def _rms(x, g):
    return x * lax.rsqrt(jnp.mean(x * x, axis=-1, keepdims=True) + EPS) * g


def _dot(a, b):
    return jnp.dot(a, b, preferred_element_type=F32)


def _const_spec(shape):
    nd = len(shape)
    return pl.BlockSpec(shape, lambda *_: (0,) * nd, pipeline_mode=pl.Buffered(1))


def _ada_kernel(cc_ref, w_ref, b_ref, o_ref):
    cc = cc_ref[...]
    s = (cc * _sigmoid(cc)).astype(BF16)
    o_ref[0] = _dot(s, w_ref[0].astype(BF16)) + b_ref[0]


def _ada(cc, ada_w, ada_b):
    depth, _, n = ada_w.shape
    return pl.pallas_call(
        _ada_kernel,
        grid=(depth, n // ADA_COLS),
        in_specs=[
            pl.BlockSpec((MOD_ROWS, D_MODEL), lambda l, j: (0, 0)),
            pl.BlockSpec((1, D_MODEL, ADA_COLS), lambda l, j: (l, 0, j)),
            pl.BlockSpec((1, 1, ADA_COLS), lambda l, j: (l, 0, j)),
        ],
        out_specs=pl.BlockSpec((1, MOD_ROWS, ADA_COLS), lambda l, j: (l, 0, j)),
        out_shape=jax.ShapeDtypeStruct((depth, MOD_ROWS, n), F32),
        compiler_params=pltpu.CompilerParams(
            dimension_semantics=("arbitrary", "arbitrary"), vmem_limit_bytes=VMEM_LIMIT_BYTES),
        name="ada",
    )(cc, ada_w, ada_b.reshape(depth, 1, n))


def _proj_kernel(x_ref, mod_ref, g_ref, w_ref, *out_refs):
    m = mod_ref[0]
    h = _rms(x_ref[0], g_ref[...]) * (1.0 + m[1:2]) + m[0:1]
    hb = h.astype(BF16)
    for j, o_ref in enumerate(out_refs):
        o_ref[0] = _dot(hb, w_ref[:, j * D_LRU:(j + 1) * D_LRU])


def _proj(x, mod, g, w_in_b, *, n_out, mod_row, rows):
    bsz, seq, _ = x.shape
    mod_map = (lambda b, t: (b, 0, 0)) if mod_row is None else (lambda b, t: (mod_row, 0, 0))
    piece = jax.ShapeDtypeStruct((bsz, seq, D_LRU), F32)
    return pl.pallas_call(
        _proj_kernel,
        grid=(bsz, seq // rows),
        in_specs=[
            pl.BlockSpec((1, rows, D_MODEL), lambda b, t: (b, t, 0)),
            pl.BlockSpec((1, N_MOD, D_MODEL), mod_map),
            _const_spec((1, D_MODEL)),
            _const_spec((D_MODEL, n_out * D_LRU)),
        ],
        out_specs=[pl.BlockSpec((1, rows, D_LRU), lambda b, t: (b, t, 0))] * n_out,
        out_shape=[piece] * n_out,
        compiler_params=pltpu.CompilerParams(
            dimension_semantics=("arbitrary", "arbitrary"), vmem_limit_bytes=VMEM_LIMIT_BYTES),
        name="proj",
    )(x, mod, g.reshape(1, D_MODEL), w_in_b[:, :n_out * D_LRU])


def _tile_scan(a, b, row, reverse):
    for s in (1, 2, 4):
        if reverse:
            keep = row < SUBLANES - s
            shift = SUBLANES - s
        else:
            keep = row >= s
            shift = s
        a_n = jnp.where(keep, pltpu.roll(a, shift, 0), 1.0)
        b_n = jnp.where(keep, pltpu.roll(b, shift, 0), 0.0)
        b = a * b_n + b
        a = a * a_n
    return a, b


def _lru_kernel(ul_ref, uc_ref, c4w_ref, c4b_ref, wg_ref, gb_ref, lam_ref, *rest, seq, ctx_len, ctx_out):
    if ctx_out:
        ol_ref, oc_ref, upad, a_s, b_s = rest
    else:
        ol_ref, upad, a_s, b_s = rest
        oc_ref = None
    pad = SUBLANES
    ctx_off = pad
    lat_off = ctx_off + ctx_len + pad
    zeros = jnp.zeros((pad, D_LRU), F32)
    upad[0:pad] = zeros
    upad[ctx_off + ctx_len:lat_off] = zeros
    upad[lat_off + seq:lat_off + seq + pad] = zeros
    upad[ctx_off:ctx_off + ctx_len] = uc_ref[0]
    upad[lat_off:lat_off + seq] = ul_ref[0]

    w4 = c4w_ref[...]
    c4b = c4b_ref[...]
    row = lax.broadcasted_iota(jnp.int32, (SUBLANES, D_LRU), 0)
    n_ctx_chunks = ctx_len // COEFF_ROWS
    n_chunks = (ctx_len + seq) // COEFF_ROWS
    n_ctx_tiles = ctx_len // SUBLANES
    n_lat_tiles = seq // SUBLANES

    for d, reverse in enumerate((False, True)):
        z = -lam_ref[d:d + 1, :]
        softplus = jnp.maximum(z, 0.0) + jnp.log(1.0 + jnp.exp(-jnp.abs(z)))
        decay = (-RG_C) * softplus
        bias_a = gb_ref[d, 0:1, :]
        bias_x = gb_ref[d, 1:2, :]

        def coeff_chunk(c, carry):
            row0 = pl.multiple_of(c * COEFF_ROWS, COEFF_ROWS)
            off = pl.multiple_of(row0 + ctx_off + jnp.where(c >= n_ctx_chunks, pad, 0), SUBLANES)
            n_ext = COEFF_ROWS + 2 * pad
            ext = upad[pl.ds(off - pad, n_ext), :]

            def shifted(k):
                r = ext if k == 0 else pltpu.roll(ext, (n_ext - k) % n_ext, 0)
                return r[pad:pad + COEFF_ROWS]

            v = (w4[0:1] * shifted(-1) + w4[1:2] * shifted(0) + w4[2:3] * shifted(1)
                 + w4[3:4] * shifted(2)) + c4b
            vb = v.astype(BF16)
            g0 = _dot(vb[:, :GATE_BLOCK], wg_ref[d, 0])
            g1 = _dot(vb[:, GATE_BLOCK:], wg_ref[d, 1])
            pre_a = jnp.concatenate([g0[:, :GATE_BLOCK], g1[:, :GATE_BLOCK]], axis=1) + bias_a
            pre_x = jnp.concatenate([g0[:, GATE_BLOCK:], g1[:, GATE_BLOCK:]], axis=1) + bias_x
            a = jnp.exp(decay * _sigmoid(pre_a))
            b = jnp.sqrt(1.0 - a * a) * (_sigmoid(pre_x) * v)
            a_s[pl.ds(row0, COEFF_ROWS), :] = a
            b_s[pl.ds(row0, COEFF_ROWS), :] = b
            return carry

        lax.fori_loop(0, n_chunks, coeff_chunk, 0)

        def scan_tiles(n_tiles, base_row, out_ref, accumulate, h0):
            def body(i, h_prev):
                t = (n_tiles - 1 - i) if reverse else i
                r_out = pl.multiple_of(t * SUBLANES, SUBLANES)
                r_in = pl.multiple_of(base_row + t * SUBLANES, SUBLANES)
                a, b = _tile_scan(a_s[pl.ds(r_in, SUBLANES), :], b_s[pl.ds(r_in, SUBLANES), :], row, reverse)
                h = b + a * h_prev
                if out_ref is not None:
                    if accumulate:
                        out_ref[0, pl.ds(r_out, SUBLANES), :] = out_ref[0, pl.ds(r_out, SUBLANES), :] + h
                    else:
                        out_ref[0, pl.ds(r_out, SUBLANES), :] = h
                return h[0:1] if reverse else h[SUBLANES - 1:SUBLANES]
            return lax.fori_loop(0, n_tiles, body, h0, unroll=SCAN_UNROLL)

        h_ctx = scan_tiles(n_ctx_tiles, 0, oc_ref, reverse, jnp.zeros((1, D_LRU), F32))
        scan_tiles(n_lat_tiles, ctx_len, ol_ref, reverse, h_ctx)


def _lru(u_lat, u_ctx, c4w, c4b, wg, gb, lam, *, ctx_out):
    bsz, seq, _ = u_lat.shape
    ctx_len = u_ctx.shape[1]
    out_shape = [jax.ShapeDtypeStruct((bsz, seq, D_LRU), F32)]
    out_specs = [pl.BlockSpec((1, seq, D_LRU), lambda b: (b, 0, 0))]
    if ctx_out:
        out_shape.append(jax.ShapeDtypeStruct((bsz, ctx_len, D_LRU), F32))
        out_specs.append(pl.BlockSpec((1, ctx_len, D_LRU), lambda b: (b, 0, 0)))
    return pl.pallas_call(
        functools.partial(_lru_kernel, seq=seq, ctx_len=ctx_len, ctx_out=ctx_out),
        grid=(bsz,),
        in_specs=[
            pl.BlockSpec((1, seq, D_LRU), lambda b: (b, 0, 0)),
            pl.BlockSpec((1, ctx_len, D_LRU), lambda b: (b, 0, 0)),
            _const_spec(c4w.shape),
            _const_spec((1, D_LRU)),
            _const_spec(wg.shape),
            _const_spec(gb.shape),
            _const_spec(lam.shape),
        ],
        out_specs=out_specs,
        out_shape=out_shape,
        scratch_shapes=[
            pltpu.VMEM((seq + ctx_len + 3 * SUBLANES, D_LRU), F32),
            pltpu.VMEM((seq + ctx_len, D_LRU), F32),
            pltpu.VMEM((seq + ctx_len, D_LRU), F32),
        ],
        compiler_params=pltpu.CompilerParams(
            dimension_semantics=("arbitrary",), vmem_limit_bytes=VMEM_LIMIT_BYTES),
        name="lru",
    )(u_lat, u_ctx, c4w, c4b.reshape(1, D_LRU), wg, gb, lam)


def _post_kernel(*refs, rows, is_ctx, final_norm):
    refs = list(refs)
    x_ref, hs_ref, gt_ref, xc_ref, bg_ref, cg_ref = refs[:6]
    refs = refs[6:]
    if not is_ctx:
        xcp_ref, cgp_ref, xcn_ref, cgn_ref = refs[:4]
        refs = refs[4:]
    mod_ref, c3w_ref, gl_ref, gc_ref, wo_ref, n2g_ref, w1_ref, w2_ref = refs[:8]
    refs = refs[8:]
    if final_norm:
        fg_ref, o_ref = refs
    else:
        (o_ref,) = refs

    t = pl.program_id(1)
    n_t = pl.num_programs(1)
    m = mod_ref[0]
    w3 = c3w_ref[...]

    y_lru = _gelu_tanh(gt_ref[0]) * hs_ref[0]
    v = cg_ref[0] * xc_ref[0]
    if is_ctx:
        pos = lax.broadcasted_iota(jnp.int32, (rows, D_CONV), 0)
        v_m = jnp.where(pos >= 1, pltpu.roll(v, 1, 0), 0.0)
        v_p = jnp.where(pos < rows - 1, pltpu.roll(v, rows - 1, 0), 0.0)
        conv = w3[0:1] * v_m + w3[1:2] * v + w3[2:3] * v_p
    else:
        va = v[:, :HALF_CONV]
        col = lax.broadcasted_iota(jnp.int32, (rows, HALF_CONV), 0) % GRID_W
        va_m = jnp.where(col >= 1, pltpu.roll(va, 1, 0), 0.0)
        va_p = jnp.where(col < GRID_W - 1, pltpu.roll(va, rows - 1, 0), 0.0)
        conv_row = w3[0:1, :HALF_CONV] * va_m + w3[1:2, :HALF_CONV] * va + w3[2:3, :HALF_CONV] * va_p
        vb = v[:, HALF_CONV:]
        prev = jnp.where(t > 0, cgp_ref[0] * xcp_ref[0], 0.0)
        nxt = jnp.where(t < n_t - 1, cgn_ref[0] * xcn_ref[0], 0.0)
        vb_m = jnp.concatenate([prev, vb[:rows - GRID_W]], axis=0)
        vb_p = jnp.concatenate([vb[GRID_W:], nxt], axis=0)
        conv_col = w3[0:1, HALF_CONV:] * vb_m + w3[1:2, HALF_CONV:] * vb + w3[2:3, HALF_CONV:] * vb_p
        conv = jnp.concatenate([conv_row, conv_col], axis=1)
    y_conv = bg_ref[0] * conv

    yl = _rms(y_lru, gl_ref[...]).astype(BF16)
    yc = _rms(y_conv, gc_ref[...]).astype(BF16)
    mix = _dot(yl, wo_ref[0:D_LRU, :]) + _dot(yc, wo_ref[D_LRU:, :])
    x1 = x_ref[0] + m[2:3] * mix

    hb = (_rms(x1, n2g_ref[...]) * (1.0 + m[4:5]) + m[3:4]).astype(BF16)
    acc = jnp.zeros((rows, D_MODEL), F32)
    for k in range(D_FF // FF_CHUNK):
        t1 = jnp.maximum(_dot(hb, w1_ref[:, k * FF_CHUNK:(k + 1) * FF_CHUNK]), 0.0)
        acc = acc + _dot((t1 * t1).astype(BF16), w2_ref[k * FF_CHUNK:(k + 1) * FF_CHUNK, :])
    x2 = x1 + m[5:6] * acc
    if final_norm:
        x2 = _rms(x2, fg_ref[...])
    o_ref[0] = x2


def _post(x, hs, pieces, mod, c3w, g_lru, g_conv, w_out_b, n2g, w1_b, w2_b, final_g, *, mod_row, rows, is_ctx):
    bsz, seq, _ = x.shape
    gt, xc, bg, cg = pieces
    final_norm = final_g is not None
    mod_map = (lambda b, t: (b, 0, 0)) if mod_row is None else (lambda b, t: (mod_row, 0, 0))
    tile = lambda width: pl.BlockSpec((1, rows, width), lambda b, t: (b, t, 0))
    args = [x, hs, gt, xc, bg, cg]
    in_specs = [tile(D_MODEL)] + [tile(D_LRU)] * 5
    if not is_ctx:
        per = rows // GRID_W
        last_blk = seq // GRID_W - 1
        prev_spec = pl.BlockSpec((1, GRID_W, HALF_CONV), lambda b, t: (b, jnp.maximum(t * per - 1, 0), 1))
        next_spec = pl.BlockSpec((1, GRID_W, HALF_CONV), lambda b, t: (b, jnp.minimum((t + 1) * per, last_blk), 1))
        args += [xc, cg, xc, cg]
        in_specs += [prev_spec, prev_spec, next_spec, next_spec]
    args += [mod, c3w, g_lru.reshape(1, D_LRU), g_conv.reshape(1, D_CONV), w_out_b,
             n2g.reshape(1, D_MODEL), w1_b, w2_b]
    in_specs += [pl.BlockSpec((1, N_MOD, D_MODEL), mod_map), _const_spec(c3w.shape), _const_spec((1, D_LRU)),
                 _const_spec((1, D_CONV)), _const_spec(w_out_b.shape), _const_spec((1, D_MODEL)),
                 _const_spec(w1_b.shape), _const_spec(w2_b.shape)]
    if final_norm:
        args.append(final_g.reshape(1, D_MODEL))
        in_specs.append(_const_spec((1, D_MODEL)))
    return pl.pallas_call(
        functools.partial(_post_kernel, rows=rows, is_ctx=is_ctx, final_norm=final_norm),
        grid=(bsz, seq // rows),
        in_specs=in_specs,
        out_specs=tile(D_MODEL),
        out_shape=jax.ShapeDtypeStruct((bsz, seq, D_MODEL), F32),
        compiler_params=pltpu.CompilerParams(
            dimension_semantics=("arbitrary", "arbitrary"), vmem_limit_bytes=VMEM_LIMIT_BYTES),
        name="post",
    )(*args)


def _pack_gates(w_a, w_x):
    per = GATE_BLOCK // LRU_HEAD_DIM
    eye = jnp.eye(per, dtype=F32)

    def blocks(w):
        w = w.reshape(2, LRU_HEADS // per, per, LRU_HEAD_DIM, LRU_HEAD_DIM)
        return jnp.einsum('dkhij,hg->dkhigj', w, eye).reshape(2, LRU_HEADS // per, GATE_BLOCK, GATE_BLOCK)

    return jnp.concatenate([blocks(w_a), blocks(w_x)], axis=-1).astype(BF16)


def kernel(x, c, ctx, c_ctx, ada_w, ada_b, norm1_g, norm2_g, w_in, conv4_w, conv4_b, gate_a_w, gate_a_b,
           gate_x_w, gate_x_b, rg_lambda, conv3_w, g_out_lru, g_out_conv, w_out, w_mlp1, w_mlp2, final_g):
    depth = ada_w.shape[0]
    bsz = x.shape[0]
    ctx_len = ctx.shape[1]
    assert bsz < MOD_ROWS
    ctx_row = bsz
    cc = jnp.concatenate([c, c_ctx[None, :], jnp.zeros((MOD_ROWS - bsz - 1, D_MODEL), F32)], axis=0)
    mod = _ada(cc, ada_w, ada_b).reshape(depth, MOD_ROWS, N_MOD, D_MODEL)

    for l in range(depth):
        last = l == depth - 1
        w_in_b = w_in[l].astype(BF16)
        w_out_b = w_out[l].astype(BF16)
        w1_b = w_mlp1[l].astype(BF16)
        w2_b = w_mlp2[l].astype(BF16)
        wg = _pack_gates(gate_a_w[l], gate_x_w[l])
        gb = jnp.stack([gate_a_b[l], gate_x_b[l]], axis=1)

        u_lat = _proj(x, mod[l], norm1_g[l], w_in_b, n_out=N_PIECES, mod_row=None, rows=PROJ_ROWS)
        u_ctx = _proj(ctx, mod[l], norm1_g[l], w_in_b, n_out=1 if last else N_PIECES, mod_row=ctx_row,
                      rows=ctx_len)
        hs = _lru(u_lat[0], u_ctx[0], conv4_w[l], conv4_b[l], wg, gb, rg_lambda[l], ctx_out=not last)
        x = _post(x, hs[0], u_lat[1:], mod[l], conv3_w[l], g_out_lru[l], g_out_conv[l], w_out_b, norm2_g[l],
                  w1_b, w2_b, final_g if last else None, mod_row=None, rows=POST_ROWS, is_ctx=False)
        if not last:
            ctx = _post(ctx, hs[1], u_ctx[1:], mod[l], conv3_w[l], g_out_lru[l], g_out_conv[l], w_out_b,
                        norm2_g[l], w1_b, w2_b, None, mod_row=ctx_row, rows=ctx_len, is_ctx=True)
    return x
```

```python
import functools

import jax
import jax.numpy as jnp
from jax import lax
from jax.experimental import pallas as pl
from jax.experimental.pallas import tpu as pltpu

F32 = jnp.float32
BF16 = jnp.bfloat16

D_MODEL = 1024
D_LRU = 512
D_CONV = 512
HALF_CONV = D_CONV // 2
D_FF = 4 * D_MODEL
GRID_W = 64
LRU_HEADS = 8
LRU_HEAD_DIM = D_LRU // LRU_HEADS
N_MOD = 6
N_PIECES = 5
N_DIR = 2
RG_C = 8.0
EPS = 1e-6

SUBLANES = 8
GATE_BLOCK = 256
MOD_ROWS = 16
VMEM_LIMIT_BYTES = 56 * 1024 * 1024

ADA_COLS = 1536
PROJ_ROWS = 512
POST_ROWS = 512
FF_CHUNK = 1024
SCAN_UNROLL = 4


def _sigmoid(z):
    return 0.5 * (jnp.tanh(0.5 * z) + 1.0)


def _gelu_tanh(z):
    return 0.5 * z * (1.0 + jnp.tanh(0.7978845608028654 * (z + 0.044715 * (z * z * z))))


def _rms(x, g):
    return x * lax.rsqrt(jnp.mean(x * x, axis=-1, keepdims=True) + EPS) * g


def _dot(a, b):
    return jnp.dot(a, b, preferred_element_type=F32)


def _const_spec(shape):
    nd = len(shape)
    return pl.BlockSpec(shape, lambda *_: (0,) * nd, pipeline_mode=pl.Buffered(1))


def _ada_kernel(cc_ref, w_ref, b_ref, o_ref):
    cc = cc_ref[...]
    s = (cc * _sigmoid(cc)).astype(BF16)
    o_ref[0] = _dot(s, w_ref[0].astype(BF16)) + b_ref[0]


def _ada(cc, ada_w, ada_b):
    depth, _, n = ada_w.shape
    return pl.pallas_call(
        _ada_kernel,
        grid=(depth, n // ADA_COLS),
        in_specs=[
            pl.BlockSpec((MOD_ROWS, D_MODEL), lambda l, j: (0, 0)),
            pl.BlockSpec((1, D_MODEL, ADA_COLS), lambda l, j: (l, 0, j)),
            pl.BlockSpec((1, 1, ADA_COLS), lambda l, j: (l, 0, j)),
        ],
        out_specs=pl.BlockSpec((1, MOD_ROWS, ADA_COLS), lambda l, j: (l, 0, j)),
        out_shape=jax.ShapeDtypeStruct((depth, MOD_ROWS, n), F32),
        compiler_params=pltpu.CompilerParams(
            dimension_semantics=("arbitrary", "arbitrary"), vmem_limit_bytes=VMEM_LIMIT_BYTES),
        name="ada",
    )(cc, ada_w, ada_b.reshape(depth, 1, n))


LOG2_E = 1.4426950408889634


def _half_rate(lam):
    z = -lam
    return (-0.5 * RG_C * LOG2_E) * (jnp.maximum(z, 0.0) + jnp.log(1.0 + jnp.exp(-jnp.abs(z))))


def _conv4(ext_ref, w4, c4b, rows):
    acc = c4b
    for k in range(4):
        acc = acc + w4[k:k + 1] * ext_ref[SUBLANES + k - 1:SUBLANES + k - 1 + rows, :]
    return acc


def _lru_gates(vb, w_lo, w_hi, bias_a, bias_x):
    g_lo = _dot(vb[:, :GATE_BLOCK], w_lo)
    g_hi = _dot(vb[:, GATE_BLOCK:], w_hi)
    th_a = jnp.tanh(jnp.concatenate([g_lo[:, :GATE_BLOCK], g_hi[:, :GATE_BLOCK]], axis=1) + bias_a)
    th_x = jnp.tanh(jnp.concatenate([g_lo[:, GATE_BLOCK:], g_hi[:, GATE_BLOCK:]], axis=1) + bias_x)
    return th_a, th_x


def _lru_coeffs(half_v, th_a, th_x, half_rate):
    a = jnp.exp2(half_rate * th_a + half_rate)
    s = 1.0 - a * a
    root = jnp.where(s > 0.0, s * lax.rsqrt(s), 0.0)
    b = root * (half_v * th_x + half_v)
    return a, b


def _proj_kernel(x_ref, xp_ref, xn_ref, mod_ref, g_ref, w_ref, c4w_ref, c4b_ref, wg_ref, gb_ref, lam_ref,
                 a_ref, b_ref, *rest, rows):
    piece_refs, ext_s = rest[:-1], rest[-1]
    t = pl.program_id(1)
    n_t = pl.num_programs(1)
    m = mod_ref[0]
    xe = jnp.concatenate([xp_ref[0], x_ref[0], xn_ref[0]], axis=0)
    hb = (_rms(xe, g_ref[...]) * (1.0 + m[1:2]) + m[0:1]).astype(BF16)
    hb_c = hb[SUBLANES:SUBLANES + rows]

    def piece(j):
        piece_refs[j][0] = _dot(hb_c, w_ref[:, (j + 1) * D_LRU:(j + 2) * D_LRU])

    u_ext = _dot(hb, w_ref[:, :D_LRU])
    ext_s[0:SUBLANES] = jnp.where(t > 0, u_ext[0:SUBLANES], 0.0)
    ext_s[SUBLANES:SUBLANES + rows] = u_ext[SUBLANES:SUBLANES + rows]
    ext_s[SUBLANES + rows:] = jnp.where(t < n_t - 1, u_ext[SUBLANES + rows:], 0.0)
    for j in range(min(2, len(piece_refs))):
        piece(j)
    v = _conv4(ext_s, c4w_ref[...], c4b_ref[...], rows)
    vb = v.astype(BF16)
    half_v = 0.5 * v
    gates = [_lru_gates(vb, wg_ref[d, 0], wg_ref[d, 1], gb_ref[d, 0:1, :], gb_ref[d, 1:2, :])
             for d in range(N_DIR)]
    for j in range(2, len(piece_refs)):
        piece(j)
    for d in range(N_DIR):
        a, b = _lru_coeffs(half_v, gates[d][0], gates[d][1], _half_rate(lam_ref[d:d + 1, :]))
        a_ref[0, d] = a
        b_ref[0, d] = b


def _proj(x, mod, g, w_in_b, c4w, c4b, wg, gb, lam, *, n_pieces, mod_row, rows):
    bsz, seq, _ = x.shape
    per8 = rows // SUBLANES
    last8 = seq // SUBLANES - 1
    mod_map = (lambda b, t: (b, 0, 0)) if mod_row is None else (lambda b, t: (mod_row, 0, 0))
    coeff = jax.ShapeDtypeStruct((bsz, N_DIR, seq, D_LRU), F32)
    piece = jax.ShapeDtypeStruct((bsz, seq, D_LRU), F32)
    n_cols = (1 + n_pieces) * D_LRU
    return pl.pallas_call(
        functools.partial(_proj_kernel, rows=rows),
        grid=(bsz, seq // rows),
        in_specs=[
            pl.BlockSpec((1, rows, D_MODEL), lambda b, t: (b, t, 0)),
            pl.BlockSpec((1, SUBLANES, D_MODEL), lambda b, t: (b, jnp.maximum(t * per8 - 1, 0), 0)),
            pl.BlockSpec((1, SUBLANES, D_MODEL), lambda b, t: (b, jnp.minimum((t + 1) * per8, last8), 0)),
            pl.BlockSpec((1, N_MOD, D_MODEL), mod_map),
            _const_spec((1, D_MODEL)),
            _const_spec((D_MODEL, n_cols)),
            _const_spec(c4w.shape),
            _const_spec((1, D_LRU)),
            _const_spec(wg.shape),
            _const_spec(gb.shape),
            _const_spec(lam.shape),
        ],
        out_specs=[pl.BlockSpec((1, N_DIR, rows, D_LRU), lambda b, t: (b, 0, t, 0))] * 2
        + [pl.BlockSpec((1, rows, D_LRU), lambda b, t: (b, t, 0))] * n_pieces,
        out_shape=[coeff, coeff] + [piece] * n_pieces,
        scratch_shapes=[pltpu.VMEM((rows + 2 * SUBLANES, D_LRU), F32)],
        compiler_params=pltpu.CompilerParams(
            dimension_semantics=("arbitrary", "arbitrary"), vmem_limit_bytes=VMEM_LIMIT_BYTES),
        name="proj",
    )(x, x, x, mod, g.reshape(1, D_MODEL), w_in_b[:, :n_cols], c4w, c4b.reshape(1, D_LRU), wg, gb, lam)


def _tile_scan(a, b, row, reverse):
    for s in (1, 2, 4):
        if reverse:
            keep = row < SUBLANES - s
            shift = SUBLANES - s
        else:
            keep = row >= s
            shift = s
        a_n = jnp.where(keep, pltpu.roll(a, shift, 0), 1.0)
        b_n = jnp.where(keep, pltpu.roll(b, shift, 0), 0.0)
        b = a * b_n + b
        a = a * a_n
    return a, b


def _scan_kernel(al_ref, bl_ref, ac_ref, bc_ref, *out_refs, seq, ctx_len, ctx_out):
    ol_ref = out_refs[0]
    oc_ref = out_refs[1] if ctx_out else None
    row = lax.broadcasted_iota(jnp.int32, (SUBLANES, D_LRU), 0)

    def scan_tiles(a_ref, b_ref, n_tiles, out_ref, reverse, h0):
        def body(i, h_prev):
            t = (n_tiles - 1 - i) if reverse else i
            r = pl.multiple_of(t * SUBLANES, SUBLANES)
            a, b = _tile_scan(a_ref[0, 0, pl.ds(r, SUBLANES), :], b_ref[0, 0, pl.ds(r, SUBLANES), :], row, reverse)
            h = b + a * h_prev
            if out_ref is not None:
                if reverse:
                    out_ref[0, pl.ds(r, SUBLANES), :] = out_ref[0, pl.ds(r, SUBLANES), :] + h
                else:
                    out_ref[0, pl.ds(r, SUBLANES), :] = h
            return h[0:1] if reverse else h[SUBLANES - 1:SUBLANES]
        return lax.fori_loop(0, n_tiles, body, h0, unroll=SCAN_UNROLL)

    for d, reverse in enumerate((False, True)):
        @pl.when(pl.program_id(1) == d)
        def _():
            h_ctx = scan_tiles(ac_ref, bc_ref, ctx_len // SUBLANES, oc_ref, reverse, jnp.zeros((1, D_LRU), F32))
            scan_tiles(al_ref, bl_ref, seq // SUBLANES, ol_ref, reverse, h_ctx)


def _scan(a_lat, b_lat, a_ctx, b_ctx, *, ctx_out):
    bsz, _, seq, _ = a_lat.shape
    ctx_len = a_ctx.shape[2]
    out_shape = [jax.ShapeDtypeStruct((bsz, seq, D_LRU), F32)]
    out_specs = [pl.BlockSpec((1, seq, D_LRU), lambda b, d: (b, 0, 0))]
    if ctx_out:
        out_shape.append(jax.ShapeDtypeStruct((bsz, ctx_len, D_LRU), F32))
        out_specs.append(pl.BlockSpec((1, ctx_len, D_LRU), lambda b, d: (b, 0, 0)))
    lat = pl.BlockSpec((1, 1, seq, D_LRU), lambda b, d: (b, d, 0, 0))
    cx = pl.BlockSpec((1, 1, ctx_len, D_LRU), lambda b, d: (b, d, 0, 0))
    return pl.pallas_call(
        functools.partial(_scan_kernel, seq=seq, ctx_len=ctx_len, ctx_out=ctx_out),
        grid=(bsz, N_DIR),
        in_specs=[lat, lat, cx, cx],
        out_specs=out_specs,
        out_shape=out_shape,
        compiler_params=pltpu.CompilerParams(
            dimension_semantics=("arbitrary", "arbitrary"), vmem_limit_bytes=VMEM_LIMIT_BYTES),
        name="scan",
    )(a_lat, b_lat, a_ctx, b_ctx)


def _post_kernel(*refs, rows, is_ctx, final_norm):
    refs = list(refs)
    x_ref, hs_ref, gt_ref, xc_ref, bg_ref, cg_ref = refs[:6]
    refs = refs[6:]
    if not is_ctx:
        xcp_ref, cgp_ref, xcn_ref, cgn_ref = refs[:4]
        refs = refs[4:]
    mod_ref, c3w_ref, gl_ref, gc_ref, wo_ref, n2g_ref, w1_ref, w2_ref = refs[:8]
    refs = refs[8:]
    if final_norm:
        fg_ref, o_ref = refs
    else:
        (o_ref,) = refs

    t = pl.program_id(1)
    n_t = pl.num_programs(1)
    m = mod_ref[0]
    w3 = c3w_ref[...]

    y_lru = _gelu_tanh(gt_ref[0]) * hs_ref[0]
    v = cg_ref[0] * xc_ref[0]
    if is_ctx:
        pos = lax.broadcasted_iota(jnp.int32, (rows, D_CONV), 0)
        v_m = jnp.where(pos >= 1, pltpu.roll(v, 1, 0), 0.0)
        v_p = jnp.where(pos < rows - 1, pltpu.roll(v, rows - 1, 0), 0.0)
        conv = w3[0:1] * v_m + w3[1:2] * v + w3[2:3] * v_p
    else:
        va = v[:, :HALF_CONV]
        col = lax.broadcasted_iota(jnp.int32, (rows, HALF_CONV), 0) % GRID_W
        va_m = jnp.where(col >= 1, pltpu.roll(va, 1, 0), 0.0)
        va_p = jnp.where(col < GRID_W - 1, pltpu.roll(va, rows - 1, 0), 0.0)
        conv_row = w3[0:1, :HALF_CONV] * va_m + w3[1:2, :HALF_CONV] * va + w3[2:3, :HALF_CONV] * va_p
        vb = v[:, HALF_CONV:]
        prev = jnp.where(t > 0, cgp_ref[0] * xcp_ref[0], 0.0)
        nxt = jnp.where(t < n_t - 1, cgn_ref[0] * xcn_ref[0], 0.0)
        vb_m = jnp.concatenate([prev, vb[:rows - GRID_W]], axis=0)
        vb_p = jnp.concatenate([vb[GRID_W:], nxt], axis=0)
        conv_col = w3[0:1, HALF_CONV:] * vb_m + w3[1:2, HALF_CONV:] * vb + w3[2:3, HALF_CONV:] * vb_p
        conv = jnp.concatenate([conv_row, conv_col], axis=1)
    y_conv = bg_ref[0] * conv

    yl = _rms(y_lru, gl_ref[...]).astype(BF16)
    yc = _rms(y_conv, gc_ref[...]).astype(BF16)
    mix = _dot(yl, wo_ref[0:D_LRU, :]) + _dot(yc, wo_ref[D_LRU:, :])
    x1 = x_ref[0] + m[2:3] * mix

    hb = (_rms(x1, n2g_ref[...]) * (1.0 + m[4:5]) + m[3:4]).astype(BF16)
    acc = jnp.zeros((rows, D_MODEL), F32)
    for k in range(D_FF // FF_CHUNK):
        t1 = jnp.maximum(_dot(hb, w1_ref[:, k * FF_CHUNK:(k + 1) * FF_CHUNK]), 0.0)
        acc = acc + _dot((t1 * t1).astype(BF16), w2_ref[k * FF_CHUNK:(k + 1) * FF_CHUNK, :])
    x2 = x1 + m[5:6] * acc
    if final_norm:
        x2 = _rms(x2, fg_ref[...])
    o_ref[0] = x2


def _post(x, hs, pieces, mod, c3w, g_lru, g_conv, w_out_b, n2g, w1_b, w2_b, final_g, *, mod_row, rows, is_ctx):
    bsz, seq, _ = x.shape
    gt, xc, bg, cg = pieces
    final_norm = final_g is not None
    mod_map = (lambda b, t: (b, 0, 0)) if mod_row is None else (lambda b, t: (mod_row, 0, 0))
    tile = lambda width: pl.BlockSpec((1, rows, width), lambda b, t: (b, t, 0))
    args = [x, hs, gt, xc, bg, cg]
    in_specs = [tile(D_MODEL)] + [tile(D_LRU)] * 5
    if not is_ctx:
        per = rows // GRID_W
        last_blk = seq // GRID_W - 1
        prev_spec = pl.BlockSpec((1, GRID_W, HALF_CONV), lambda b, t: (b, jnp.maximum(t * per - 1, 0), 1))
        next_spec = pl.BlockSpec((1, GRID_W, HALF_CONV), lambda b, t: (b, jnp.minimum((t + 1) * per, last_blk), 1))
        args += [xc, cg, xc, cg]
        in_specs += [prev_spec, prev_spec, next_spec, next_spec]
    args += [mod, c3w, g_lru.reshape(1, D_LRU), g_conv.reshape(1, D_CONV), w_out_b,
             n2g.reshape(1, D_MODEL), w1_b, w2_b]
    in_specs += [pl.BlockSpec((1, N_MOD, D_MODEL), mod_map), _const_spec(c3w.shape), _const_spec((1, D_LRU)),
                 _const_spec((1, D_CONV)), _const_spec(w_out_b.shape), _const_spec((1, D_MODEL)),
                 _const_spec(w1_b.shape), _const_spec(w2_b.shape)]
    if final_norm:
        args.append(final_g.reshape(1, D_MODEL))
        in_specs.append(_const_spec((1, D_MODEL)))
    return pl.pallas_call(
        functools.partial(_post_kernel, rows=rows, is_ctx=is_ctx, final_norm=final_norm),
        grid=(bsz, seq // rows),
        in_specs=in_specs,
        out_specs=tile(D_MODEL),
        out_shape=jax.ShapeDtypeStruct((bsz, seq, D_MODEL), F32),
        compiler_params=pltpu.CompilerParams(
            dimension_semantics=("arbitrary", "arbitrary"), vmem_limit_bytes=VMEM_LIMIT_BYTES),
        name="post",
    )(*args)


def _pack_gates(w_a, w_x):
    per = GATE_BLOCK // LRU_HEAD_DIM
    eye = jnp.eye(per, dtype=F32)

    def blocks(w):
        w = w.reshape(N_DIR, LRU_HEADS // per, per, LRU_HEAD_DIM, LRU_HEAD_DIM)
        return jnp.einsum('dkhij,hg->dkhigj', w, eye).reshape(N_DIR, LRU_HEADS // per, GATE_BLOCK, GATE_BLOCK)

    return (0.5 * jnp.concatenate([blocks(w_a), blocks(w_x)], axis=-1).astype(BF16)).astype(BF16)


def kernel(x, c, ctx, c_ctx, ada_w, ada_b, norm1_g, norm2_g, w_in, conv4_w, conv4_b, gate_a_w, gate_a_b,
           gate_x_w, gate_x_b, rg_lambda, conv3_w, g_out_lru, g_out_conv, w_out, w_mlp1, w_mlp2, final_g):
    depth = ada_w.shape[0]
    bsz = x.shape[0]
    ctx_len = ctx.shape[1]
    assert bsz < MOD_ROWS
    ctx_row = bsz
    cc = jnp.concatenate([c, c_ctx[None, :], jnp.zeros((MOD_ROWS - bsz - 1, D_MODEL), F32)], axis=0)
    mod = _ada(cc, ada_w, ada_b).reshape(depth, MOD_ROWS, N_MOD, D_MODEL)

    for l in range(depth):
        last = l == depth - 1
        w_in_b = w_in[l].astype(BF16)
        w_out_b = w_out[l].astype(BF16)
        w1_b = w_mlp1[l].astype(BF16)
        w2_b = w_mlp2[l].astype(BF16)
        wg = _pack_gates(gate_a_w[l], gate_x_w[l])
        gb = 0.5 * jnp.stack([gate_a_b[l], gate_x_b[l]], axis=1)
        lru_params = (conv4_w[l], conv4_b[l], wg, gb, rg_lambda[l])

        lat = _proj(x, mod[l], norm1_g[l], w_in_b, *lru_params, n_pieces=N_PIECES - 1, mod_row=None,
                    rows=PROJ_ROWS)
        cx = _proj(ctx, mod[l], norm1_g[l], w_in_b, *lru_params, n_pieces=0 if last else N_PIECES - 1,
                   mod_row=ctx_row, rows=ctx_len)
        hs = _scan(lat[0], lat[1], cx[0], cx[1], ctx_out=not last)
        x = _post(x, hs[0], lat[2:], mod[l], conv3_w[l], g_out_lru[l], g_out_conv[l], w_out_b, norm2_g[l],
                  w1_b, w2_b, final_g if last else None, mod_row=None, rows=POST_ROWS, is_ctx=False)
        if not last:
            ctx = _post(ctx, hs[1], cx[2:], mod[l], conv3_w[l], g_out_lru[l], g_out_conv[l], w_out_b,
                        norm2_g[l], w1_b, w2_b, None, mod_row=ctx_row, rows=ctx_len, is_ctx=True)
    return x
```

```python
import functools

import jax
import jax.numpy as jnp
from jax import lax
from jax.experimental import pallas as pl
from jax.experimental.pallas import tpu as pltpu

F32 = jnp.float32
BF16 = jnp.bfloat16

D_MODEL = 1024
D_LRU = 512
D_CONV = 512
HALF_CONV = D_CONV // 2
D_FF = 4 * D_MODEL
GRID_W = 64
LRU_HEADS = 8
LRU_HEAD_DIM = D_LRU // LRU_HEADS
N_MOD = 6
N_PIECES = 4
N_DIR = 2
RG_C = 8.0
EPS = 1e-6
LOG2_E = 1.4426950408889634

BATCH = 8
GATE_BLOCK = 256
MOD_ROWS = 16
VMEM_LIMIT_BYTES = 56 * 1024 * 1024

ADA_COLS = 1536
TILE_STEPS = GRID_W
TILE_ROWS = TILE_STEPS * BATCH
FF_CHUNK = 1024


def _sigmoid(z):
    return 0.5 * (jnp.tanh(0.5 * z) + 1.0)


def _gelu_tanh(z):
    return 0.5 * z * (1.0 + jnp.tanh(0.7978845608028654 * (z + 0.044715 * (z * z * z))))


def _rms(x, g):
    return x * lax.rsqrt(jnp.mean(x * x, axis=-1, keepdims=True) + EPS) * g


def _dot(a, b):
    return jnp.dot(a, b, preferred_element_type=F32)


def _per_batch(x, fn):
    rows, d = x.shape
    return fn(x.reshape(rows // BATCH, BATCH, d)).reshape(rows, d)


def _const_spec(shape):
    nd = len(shape)
    return pl.BlockSpec(shape, lambda *_: (0,) * nd, pipeline_mode=pl.Buffered(1))


def _ada_kernel(cc_ref, w_ref, b_ref, o_ref):
    cc = cc_ref[...]
    s = (cc * _sigmoid(cc)).astype(BF16)
    o_ref[0] = _dot(s, w_ref[0].astype(BF16)) + b_ref[0]


def _ada(cc, ada_w, ada_b):
    depth, _, n = ada_w.shape
    return pl.pallas_call(
        _ada_kernel,
        grid=(depth, n // ADA_COLS),
        in_specs=[
            pl.BlockSpec((MOD_ROWS, D_MODEL), lambda l, j: (0, 0)),
            pl.BlockSpec((1, D_MODEL, ADA_COLS), lambda l, j: (l, 0, j)),
            pl.BlockSpec((1, 1, ADA_COLS), lambda l, j: (l, 0, j)),
        ],
        out_specs=pl.BlockSpec((1, MOD_ROWS, ADA_COLS), lambda l, j: (l, 0, j)),
        out_shape=jax.ShapeDtypeStruct((depth, MOD_ROWS, n), F32),
        compiler_params=pltpu.CompilerParams(
            dimension_semantics=("arbitrary", "arbitrary"), vmem_limit_bytes=VMEM_LIMIT_BYTES),
        name="ada",
    )(cc, ada_w, ada_b.reshape(depth, 1, n))


def _half_rate(lam):
    z = -lam
    return (-0.5 * RG_C * LOG2_E) * (jnp.maximum(z, 0.0) + jnp.log(1.0 + jnp.exp(-jnp.abs(z))))


def _lru_gates(vb, w_lo, w_hi, bias_a, bias_x):
    g_lo = _dot(vb[:, :GATE_BLOCK], w_lo)
    g_hi = _dot(vb[:, GATE_BLOCK:], w_hi)
    th_a = jnp.tanh(jnp.concatenate([g_lo[:, :GATE_BLOCK], g_hi[:, :GATE_BLOCK]], axis=1) + bias_a)
    th_x = jnp.tanh(jnp.concatenate([g_lo[:, GATE_BLOCK:], g_hi[:, GATE_BLOCK:]], axis=1) + bias_x)
    return th_a, th_x


def _lru_coeffs(half_v, th_a, th_x, half_rate):
    a = jnp.exp2(half_rate * th_a + half_rate)
    s = 1.0 - a * a
    root = jnp.where(s > 0.0, s * lax.rsqrt(s), 0.0)
    b = root * (half_v * th_x + half_v)
    return a, b


def _recurrence(a, b, h, reverse):
    steps = a.shape[0] // BATCH
    out = [None] * steps
    for t in (reversed(range(steps)) if reverse else range(steps)):
        sl = slice(t * BATCH, (t + 1) * BATCH)
        h = a[sl] * h + b[sl]
        out[t] = h
    return jnp.concatenate(out, axis=0), h


def _proj_kernel(x_ref, xp_ref, xn_ref, mod_ref, g_ref, w_ref, c4w_ref, c4b_ref, wg_ref, gb_ref, lam_ref, h0_ref,
                 hf_ref, ab_ref, bb_ref, hend_ref, *rest):
    piece_refs, carry_s = rest[:-1], rest[-1]
    j = pl.program_id(0)
    n_j = pl.num_programs(0)
    rows = x_ref.shape[0]
    m = mod_ref[...]
    xe = jnp.concatenate([xp_ref[...], x_ref[...], xn_ref[...]], axis=0)
    y = _rms(xe, g_ref[...])
    hb = _per_batch(y, lambda y3: y3 * (1.0 + m[1])[None] + m[0][None]).astype(BF16)
    hb_c = hb[BATCH:BATCH + rows]

    def piece(k):
        piece_refs[k][...] = _dot(hb_c, w_ref[:, (k + 1) * D_LRU:(k + 2) * D_LRU])

    u_ext = _dot(hb, w_ref[:, :D_LRU])
    u_ext = jnp.concatenate([jnp.where(j > 0, u_ext[:BATCH], 0.0), u_ext[BATCH:BATCH + rows],
                             jnp.where(j < n_j - 1, u_ext[BATCH + rows:], 0.0)], axis=0)
    for k in range(min(2, len(piece_refs))):
        piece(k)
    w4 = c4w_ref[...]
    v = c4b_ref[...]
    for k in range(4):
        v = v + w4[k:k + 1] * u_ext[k * BATCH:k * BATCH + rows]
    vb = v.astype(BF16)
    half_v = 0.5 * v
    gates = [_lru_gates(vb, wg_ref[d, 0], wg_ref[d, 1], gb_ref[d, 0:1, :], gb_ref[d, 1:2, :])
             for d in range(N_DIR)]
    for k in range(2, len(piece_refs)):
        piece(k)
    a_f, b_f = _lru_coeffs(half_v, gates[0][0], gates[0][1], _half_rate(lam_ref[0:1, :]))
    a_b, b_b = _lru_coeffs(half_v, gates[1][0], gates[1][1], _half_rate(lam_ref[1:2, :]))
    ab_ref[...] = a_b
    bb_ref[...] = b_b

    @pl.when(j == 0)
    def _():
        carry_s[...] = h0_ref[...]

    h_f, h_last = _recurrence(a_f, b_f, carry_s[...], False)
    hf_ref[...] = h_f
    carry_s[...] = h_last
    hend_ref[...] = h_last


def _proj(x, mod, g, w_in_b, c4w, c4b, wg, gb, lam, h0, *, n_pieces):
    n_rows = x.shape[0]
    n_tiles = n_rows // TILE_ROWS
    per_prev = TILE_ROWS // BATCH
    per_next = TILE_ROWS // (2 * BATCH)
    last_next = n_rows // (2 * BATCH) - 1
    n_cols = (1 + n_pieces) * D_LRU
    tile = pl.BlockSpec((TILE_ROWS, D_LRU), lambda j: (j, 0))
    stream = jax.ShapeDtypeStruct((n_rows, D_LRU), F32)
    state = jax.ShapeDtypeStruct((BATCH, D_LRU), F32)
    return pl.pallas_call(
        _proj_kernel,
        grid=(n_tiles,),
        in_specs=[
            pl.BlockSpec((TILE_ROWS, D_MODEL), lambda j: (j, 0)),
            pl.BlockSpec((BATCH, D_MODEL), lambda j: (jnp.maximum(j * per_prev - 1, 0), 0)),
            pl.BlockSpec((2 * BATCH, D_MODEL), lambda j: (jnp.minimum((j + 1) * per_next, last_next), 0)),
            _const_spec(mod.shape),
            _const_spec((1, D_MODEL)),
            _const_spec((D_MODEL, n_cols)),
            _const_spec(c4w.shape),
            _const_spec((1, D_LRU)),
            _const_spec(wg.shape),
            _const_spec(gb.shape),
            _const_spec(lam.shape),
            _const_spec((BATCH, D_LRU)),
        ],
        out_specs=[tile, tile, tile, pl.BlockSpec((BATCH, D_LRU), lambda j: (0, 0))] + [tile] * n_pieces,
        out_shape=[stream, stream, stream, state] + [stream] * n_pieces,
        scratch_shapes=[pltpu.VMEM((BATCH, D_LRU), F32)],
        compiler_params=pltpu.CompilerParams(
            dimension_semantics=("arbitrary",), vmem_limit_bytes=VMEM_LIMIT_BYTES),
        name="proj",
    )(x, x, x, mod, g.reshape(1, D_MODEL), w_in_b[:, :n_cols], c4w, c4b.reshape(1, D_LRU), wg, gb, lam, h0)


def _ctx_state_kernel(a_ref, b_ref, o_ref):
    _, o_ref[...] = _recurrence(a_ref[...], b_ref[...], jnp.zeros((BATCH, D_LRU), F32), True)


def _ctx_state(a_b, b_b):
    n_rows = a_b.shape[0]
    return pl.pallas_call(
        _ctx_state_kernel,
        grid=(1,),
        in_specs=[pl.BlockSpec((n_rows, D_LRU), lambda i: (0, 0))] * 2,
        out_specs=pl.BlockSpec((BATCH, D_LRU), lambda i: (0, 0)),
        out_shape=jax.ShapeDtypeStruct((BATCH, D_LRU), F32),
        compiler_params=pltpu.CompilerParams(
            dimension_semantics=("arbitrary",), vmem_limit_bytes=VMEM_LIMIT_BYTES),
        name="ctx_state",
    )(a_b, b_b)


def _post_kernel(*refs, is_ctx, final_norm):
    refs = list(refs)
    x_ref, hf_ref, ab_ref, bb_ref, gt_ref, xc_ref, bg_ref, cg_ref, xcp_ref, cgp_ref, xcn_ref, cgn_ref = refs[:12]
    mod_ref, c3w_ref, gl_ref, gc_ref, wo_ref, n2g_ref, w1_ref, w2_ref, h0_ref = refs[12:21]
    refs = refs[21:]
    fg_ref = refs.pop(0) if final_norm else None
    o_ref, hend_ref, carry_s = refs

    i = pl.program_id(0)
    n_i = pl.num_programs(0)
    j = n_i - 1 - i
    rows = x_ref.shape[0]
    m = mod_ref[...]
    w3 = c3w_ref[...]

    @pl.when(i == 0)
    def _():
        carry_s[...] = h0_ref[...]

    h_b, h_last = _recurrence(ab_ref[...], bb_ref[...], carry_s[...], True)
    carry_s[...] = h_last
    hend_ref[...] = h_last
    y_lru = _gelu_tanh(gt_ref[...]) * (hf_ref[...] + h_b)

    v = cg_ref[...] * xc_ref[...]
    zeros = jnp.zeros((BATCH, D_CONV), F32)
    if is_ctx:
        before = jnp.where(j > 0, cgp_ref[...] * xcp_ref[...], 0.0)
        after = jnp.where(j < n_i - 1, cgn_ref[...] * xcn_ref[...], 0.0)
        v_m = jnp.concatenate([before, v[:rows - BATCH]], axis=0)
        v_p = jnp.concatenate([v[BATCH:], after], axis=0)
        conv = w3[0:1] * v_m + w3[1:2] * v + w3[2:3] * v_p
    else:
        va = v[:, :HALF_CONV]
        va_m = jnp.concatenate([zeros[:, :HALF_CONV], va[:rows - BATCH]], axis=0)
        va_p = jnp.concatenate([va[BATCH:], zeros[:, :HALF_CONV]], axis=0)
        conv_row = w3[0:1, :HALF_CONV] * va_m + w3[1:2, :HALF_CONV] * va + w3[2:3, :HALF_CONV] * va_p
        vb_m = jnp.where(j > 0, cgp_ref[...] * xcp_ref[...], 0.0)
        vb_p = jnp.where(j < n_i - 1, cgn_ref[...] * xcn_ref[...], 0.0)
        conv_col = (w3[0:1, HALF_CONV:] * vb_m + w3[1:2, HALF_CONV:] * v[:, HALF_CONV:]
                    + w3[2:3, HALF_CONV:] * vb_p)
        conv = jnp.concatenate([conv_row, conv_col], axis=1)
    y_conv = bg_ref[...] * conv

    yl = _rms(y_lru, gl_ref[...]).astype(BF16)
    yc = _rms(y_conv, gc_ref[...]).astype(BF16)
    mix = _dot(yl, wo_ref[0:D_LRU, :]) + _dot(yc, wo_ref[D_LRU:, :])
    x1 = x_ref[...] + _per_batch(mix, lambda t3: t3 * m[2][None])

    y2 = _rms(x1, n2g_ref[...])
    hb = _per_batch(y2, lambda y3: y3 * (1.0 + m[4])[None] + m[3][None]).astype(BF16)
    acc = jnp.zeros((rows, D_MODEL), F32)
    for k in range(D_FF // FF_CHUNK):
        t1 = jnp.maximum(_dot(hb, w1_ref[:, k * FF_CHUNK:(k + 1) * FF_CHUNK]), 0.0)
        acc = acc + _dot((t1 * t1).astype(BF16), w2_ref[k * FF_CHUNK:(k + 1) * FF_CHUNK, :])
    x2 = x1 + _per_batch(acc, lambda t3: t3 * m[5][None])
    if final_norm:
        x2 = _rms(x2, fg_ref[...])
    o_ref[...] = x2


def _post(x, hf, ab, bb, pieces, mod, c3w, g_lru, g_conv, w_out_b, n2g, w1_b, w2_b, h0, final_g, *, is_ctx):
    n_rows = x.shape[0]
    n_tiles = n_rows // TILE_ROWS
    gt, xc, bg, cg = pieces
    final_norm = final_g is not None

    def rev(i):
        return n_tiles - 1 - i

    tile = lambda width: pl.BlockSpec((TILE_ROWS, width), lambda i: (rev(i), 0))
    if is_ctx:
        per = TILE_ROWS // BATCH
        last = n_rows // BATCH - 1
        prev_spec = pl.BlockSpec((BATCH, D_CONV), lambda i: (jnp.maximum(rev(i) * per - 1, 0), 0))
        next_spec = pl.BlockSpec((BATCH, D_CONV), lambda i: (jnp.minimum((rev(i) + 1) * per, last), 0))
    else:
        prev_spec = pl.BlockSpec((TILE_ROWS, HALF_CONV), lambda i: (jnp.maximum(rev(i) - 1, 0), 1))
        next_spec = pl.BlockSpec((TILE_ROWS, HALF_CONV), lambda i: (jnp.minimum(rev(i) + 1, n_tiles - 1), 1))
    args = [x, hf, ab, bb, gt, xc, bg, cg, xc, cg, xc, cg,
            mod, c3w, g_lru.reshape(1, D_LRU), g_conv.reshape(1, D_CONV), w_out_b, n2g.reshape(1, D_MODEL),
            w1_b, w2_b, h0]
    in_specs = ([tile(D_MODEL)] + [tile(D_LRU)] * 7 + [prev_spec, prev_spec, next_spec, next_spec]
                + [_const_spec(mod.shape), _const_spec(c3w.shape), _const_spec((1, D_LRU)),
                   _const_spec((1, D_CONV)), _const_spec(w_out_b.shape), _const_spec((1, D_MODEL)),
                   _const_spec(w1_b.shape), _const_spec(w2_b.shape), _const_spec((BATCH, D_LRU))])
    if final_norm:
        args.append(final_g.reshape(1, D_MODEL))
        in_specs.append(_const_spec((1, D_MODEL)))
    return pl.pallas_call(
        functools.partial(_post_kernel, is_ctx=is_ctx, final_norm=final_norm),
        grid=(n_tiles,),
        in_specs=in_specs,
        out_specs=[tile(D_MODEL), pl.BlockSpec((BATCH, D_LRU), lambda i: (0, 0))],
        out_shape=[jax.ShapeDtypeStruct((n_rows, D_MODEL), F32), jax.ShapeDtypeStruct((BATCH, D_LRU), F32)],
        scratch_shapes=[pltpu.VMEM((BATCH, D_LRU), F32)],
        compiler_params=pltpu.CompilerParams(
            dimension_semantics=("arbitrary",), vmem_limit_bytes=VMEM_LIMIT_BYTES),
        name="post",
    )(*args)


def _pack_gates(w_a, w_x):
    per = GATE_BLOCK // LRU_HEAD_DIM
    eye = jnp.eye(per, dtype=F32)

    def blocks(w):
        w = w.reshape(N_DIR, LRU_HEADS // per, per, LRU_HEAD_DIM, LRU_HEAD_DIM)
        return jnp.einsum('dkhij,hg->dkhigj', w, eye).reshape(N_DIR, LRU_HEADS // per, GATE_BLOCK, GATE_BLOCK)

    return (0.5 * jnp.concatenate([blocks(w_a), blocks(w_x)], axis=-1).astype(BF16)).astype(BF16)


def _time_major(x):
    bsz, seq, d = x.shape
    return jnp.transpose(x, (1, 0, 2)).reshape(seq * bsz, d)


def kernel(x, c, ctx, c_ctx, ada_w, ada_b, norm1_g, norm2_g, w_in, conv4_w, conv4_b, gate_a_w, gate_a_b,
           gate_x_w, gate_x_b, rg_lambda, conv3_w, g_out_lru, g_out_conv, w_out, w_mlp1, w_mlp2, final_g):
    depth = ada_w.shape[0]
    bsz, seq, _ = x.shape
    assert bsz == BATCH and seq % TILE_STEPS == 0 and ctx.shape[1] % TILE_STEPS == 0
    cc = jnp.concatenate([c, c_ctx[None, :], jnp.zeros((MOD_ROWS - bsz - 1, D_MODEL), F32)], axis=0)
    mod = _ada(cc, ada_w, ada_b).reshape(depth, MOD_ROWS, N_MOD, D_MODEL)
    mod_lat = jnp.transpose(mod[:, :bsz], (0, 2, 1, 3))
    mod_ctx = jnp.broadcast_to(mod[:, bsz][:, :, None, :], mod_lat.shape)

    x = _time_major(x)
    ctx = _time_major(ctx)
    zero_state = jnp.zeros((BATCH, D_LRU), F32)

    for l in range(depth):
        last = l == depth - 1
        w_in_b = w_in[l].astype(BF16)
        w_out_b = w_out[l].astype(BF16)
        w1_b = w_mlp1[l].astype(BF16)
        w2_b = w_mlp2[l].astype(BF16)
        wg = _pack_gates(gate_a_w[l], gate_x_w[l])
        gb = 0.5 * jnp.stack([gate_a_b[l], gate_x_b[l]], axis=1)
        lru_params = (conv4_w[l], conv4_b[l], wg, gb, rg_lambda[l])
        post_params = (conv3_w[l], g_out_lru[l], g_out_conv[l], w_out_b, norm2_g[l], w1_b, w2_b)

        cx = _proj(ctx, mod_ctx[l], norm1_g[l], w_in_b, *lru_params, zero_state, n_pieces=0 if last else N_PIECES)
        lat = _proj(x, mod_lat[l], norm1_g[l], w_in_b, *lru_params, cx[3], n_pieces=N_PIECES)
        if last:
            h0_bwd = _ctx_state(cx[1], cx[2])
        else:
            ctx, h0_bwd = _post(ctx, cx[0], cx[1], cx[2], cx[4:], mod_ctx[l], *post_params, zero_state, None,
                                is_ctx=True)
        x, _ = _post(x, lat[0], lat[1], lat[2], lat[4:], mod_lat[l], *post_params, h0_bwd,
                     final_g if last else None, is_ctx=False)
    return jnp.transpose(x.reshape(seq, bsz, D_MODEL), (1, 0, 2))
```

```python
import functools

import jax
import jax.numpy as jnp
from jax import lax
from jax.experimental import pallas as pl
from jax.experimental.pallas import tpu as pltpu

F32 = jnp.float32
BF16 = jnp.bfloat16

D_MODEL = 1024
D_LRU = 512
D_CONV = 512
HALF_CONV = D_CONV // 2
D_FF = 4 * D_MODEL
GRID_W = 64
LRU_HEADS = 8
LRU_HEAD_DIM = D_LRU // LRU_HEADS
N_MOD = 6
N_PIECES = 4
N_DIR = 2
RG_C = 8.0
EPS = 1e-6
LOG2_E = 1.4426950408889634

SUBLANES = 8
LANES = 128
BATCH = SUBLANES
GATE_BLOCK = 256
MOD_ROWS = 16
VMEM_LIMIT_BYTES = 56 * 1024 * 1024

ADA_COLS = 1536
TILE_STEPS = GRID_W
TILE_ROWS = TILE_STEPS * BATCH
FF_CHUNK = 1024


def _sigmoid(z):
    return 0.5 * (jnp.tanh(0.5 * z) + 1.0)


def _gelu_tanh(z):
    return 0.5 * z * (1.0 + jnp.tanh(0.7978845608028654 * (z + 0.044715 * (z * z * z))))


def _rms(x, g):
    return x * lax.rsqrt(jnp.mean(x * x, axis=-1, keepdims=True) + EPS) * g


def _dot(a, b):
    return jnp.dot(a, b, preferred_element_type=F32)


def _per_batch(x, fn):
    rows, d = x.shape
    return fn(x.reshape(rows // BATCH, BATCH, d)).reshape(rows, d)


def _const_spec(shape):
    nd = len(shape)
    return pl.BlockSpec(shape, lambda *_: (0,) * nd, pipeline_mode=pl.Buffered(1))


def _ada_kernel(cc_ref, w_ref, b_ref, o_ref):
    cc = cc_ref[...]
    s = (cc * _sigmoid(cc)).astype(BF16)
    o_ref[0] = _dot(s, w_ref[0].astype(BF16)) + b_ref[0]


def _ada(cc, ada_w, ada_b):
    depth, _, n = ada_w.shape
    return pl.pallas_call(
        _ada_kernel,
        grid=(depth, n // ADA_COLS),
        in_specs=[
            pl.BlockSpec((MOD_ROWS, D_MODEL), lambda l, j: (0, 0)),
            pl.BlockSpec((1, D_MODEL, ADA_COLS), lambda l, j: (l, 0, j)),
            pl.BlockSpec((1, 1, ADA_COLS), lambda l, j: (l, 0, j)),
        ],
        out_specs=pl.BlockSpec((1, MOD_ROWS, ADA_COLS), lambda l, j: (l, 0, j)),
        out_shape=jax.ShapeDtypeStruct((depth, MOD_ROWS, n), F32),
        compiler_params=pltpu.CompilerParams(
            dimension_semantics=("arbitrary", "arbitrary"), vmem_limit_bytes=VMEM_LIMIT_BYTES),
        name="ada",
    )(cc, ada_w, ada_b.reshape(depth, 1, n))


def _half_rate(lam):
    z = -lam
    return (-0.5 * RG_C * LOG2_E) * (jnp.maximum(z, 0.0) + jnp.log(1.0 + jnp.exp(-jnp.abs(z))))


def _lru_gates(vb, w_lo, w_hi, bias_a, bias_x):
    g_lo = _dot(vb[:, :GATE_BLOCK], w_lo)
    g_hi = _dot(vb[:, GATE_BLOCK:], w_hi)
    th_a = jnp.tanh(jnp.concatenate([g_lo[:, :GATE_BLOCK], g_hi[:, :GATE_BLOCK]], axis=1) + bias_a)
    th_x = jnp.tanh(jnp.concatenate([g_lo[:, GATE_BLOCK:], g_hi[:, GATE_BLOCK:]], axis=1) + bias_x)
    return th_a, th_x


def _lru_coeffs(half_v, th_a, th_x, half_rate):
    a = jnp.exp2(half_rate * th_a + half_rate)
    s = 1.0 - a * a
    root = jnp.where(s > 0.0, s * lax.rsqrt(s), 0.0)
    b = root * (half_v * th_x + half_v)
    return a, b


def _recurrence(a, b, h, reverse):
    steps = a.shape[0] // BATCH
    out = [None] * steps
    for t in (reversed(range(steps)) if reverse else range(steps)):
        sl = slice(t * BATCH, (t + 1) * BATCH)
        h = a[sl] * h + b[sl]
        out[t] = h
    return jnp.concatenate(out, axis=0), h


def _to_time_major(x_ref, xs):
    steps = x_ref.shape[1]
    for b in range(BATCH):
        for s in range(xs.shape[0]):
            xs[s, pl.ds(b, steps, stride=BATCH), :] = x_ref[b, :, s * LANES:(s + 1) * LANES]
    return jnp.concatenate([xs[s] for s in range(xs.shape[0])], axis=1)


def _proj_kernel(x_ref, xp_ref, xn_ref, mod_ref, g_ref, w_ref, c4w_ref, c4b_ref, wg_ref, gb_ref, lam_ref, h0_ref,
                 hf_ref, ab_ref, bb_ref, hend_ref, *rest, natural_in):
    if natural_in:
        piece_refs, carry_s, xs = rest[:-2], rest[-2], rest[-1]
    else:
        piece_refs, carry_s = rest[:-1], rest[-1]
    j = pl.program_id(0)
    n_j = pl.num_programs(0)
    rows = TILE_ROWS
    m = mod_ref[...]
    if natural_in:
        last = xp_ref.shape[1] - 1
        xe = jnp.concatenate([xp_ref[:, last, :], _to_time_major(x_ref, xs), xn_ref[:, 0, :], xn_ref[:, 1, :]],
                             axis=0)
    else:
        xe = jnp.concatenate([xp_ref[...], x_ref[...], xn_ref[...]], axis=0)
    y = _rms(xe, g_ref[...])
    hb = _per_batch(y, lambda y3: y3 * (1.0 + m[1])[None] + m[0][None]).astype(BF16)
    hb_c = hb[BATCH:BATCH + rows]

    def piece(k):
        piece_refs[k][...] = _dot(hb_c, w_ref[:, (k + 1) * D_LRU:(k + 2) * D_LRU])

    u_ext = _dot(hb, w_ref[:, :D_LRU])
    u_ext = jnp.concatenate([jnp.where(j > 0, u_ext[:BATCH], 0.0), u_ext[BATCH:BATCH + rows],
                             jnp.where(j < n_j - 1, u_ext[BATCH + rows:], 0.0)], axis=0)
    for k in range(min(2, len(piece_refs))):
        piece(k)
    w4 = c4w_ref[...]
    v = c4b_ref[...]
    for k in range(4):
        v = v + w4[k:k + 1] * u_ext[k * BATCH:k * BATCH + rows]
    vb = v.astype(BF16)
    half_v = 0.5 * v
    gates = [_lru_gates(vb, wg_ref[d, 0], wg_ref[d, 1], gb_ref[d, 0:1, :], gb_ref[d, 1:2, :])
             for d in range(N_DIR)]
    for k in range(2, len(piece_refs)):
        piece(k)
    a_f, b_f = _lru_coeffs(half_v, gates[0][0], gates[0][1], _half_rate(lam_ref[0:1, :]))
    a_b, b_b = _lru_coeffs(half_v, gates[1][0], gates[1][1], _half_rate(lam_ref[1:2, :]))
    ab_ref[...] = a_b
    bb_ref[...] = b_b

    @pl.when(j == 0)
    def _():
        carry_s[...] = h0_ref[...]

    h_f, h_last = _recurrence(a_f, b_f, carry_s[...], False)
    hf_ref[...] = h_f
    carry_s[...] = h_last
    hend_ref[...] = h_last


def _proj(x, mod, g, w_in_b, c4w, c4b, wg, gb, lam, h0, *, n_pieces):
    natural_in = x.ndim == 3
    n_rows = x.shape[0] * x.shape[1] if natural_in else x.shape[0]
    n_tiles = n_rows // TILE_ROWS
    n_cols = (1 + n_pieces) * D_LRU
    tile = pl.BlockSpec((TILE_ROWS, D_LRU), lambda j: (j, 0))
    stream = jax.ShapeDtypeStruct((n_rows, D_LRU), F32)
    state = jax.ShapeDtypeStruct((BATCH, D_LRU), F32)
    scratch = [pltpu.VMEM((BATCH, D_LRU), F32)]
    if natural_in:
        halo = SUBLANES
        per = TILE_STEPS // halo
        last_halo = x.shape[1] // halo - 1
        x_specs = [
            pl.BlockSpec((BATCH, TILE_STEPS, D_MODEL), lambda j: (0, j, 0)),
            pl.BlockSpec((BATCH, halo, D_MODEL), lambda j: (0, jnp.maximum(j * per - 1, 0), 0)),
            pl.BlockSpec((BATCH, halo, D_MODEL), lambda j: (0, jnp.minimum((j + 1) * per, last_halo), 0)),
        ]
        scratch.append(pltpu.VMEM((D_MODEL // LANES, TILE_ROWS, LANES), F32))
    else:
        per_prev = TILE_ROWS // BATCH
        per_next = TILE_ROWS // (2 * BATCH)
        last_next = n_rows // (2 * BATCH) - 1
        x_specs = [
            pl.BlockSpec((TILE_ROWS, D_MODEL), lambda j: (j, 0)),
            pl.BlockSpec((BATCH, D_MODEL), lambda j: (jnp.maximum(j * per_prev - 1, 0), 0)),
            pl.BlockSpec((2 * BATCH, D_MODEL), lambda j: (jnp.minimum((j + 1) * per_next, last_next), 0)),
        ]
    return pl.pallas_call(
        functools.partial(_proj_kernel, natural_in=natural_in),
        grid=(n_tiles,),
        in_specs=x_specs + [
            _const_spec(mod.shape),
            _const_spec((1, D_MODEL)),
            _const_spec((D_MODEL, n_cols)),
            _const_spec(c4w.shape),
            _const_spec((1, D_LRU)),
            _const_spec(wg.shape),
            _const_spec(gb.shape),
            _const_spec(lam.shape),
            _const_spec((BATCH, D_LRU)),
        ],
        out_specs=[tile, tile, tile, pl.BlockSpec((BATCH, D_LRU), lambda j: (0, 0))] + [tile] * n_pieces,
        out_shape=[stream, stream, stream, state] + [stream] * n_pieces,
        scratch_shapes=scratch,
        compiler_params=pltpu.CompilerParams(
            dimension_semantics=("arbitrary",), vmem_limit_bytes=VMEM_LIMIT_BYTES),
        name="proj",
    )(x, x, x, mod, g.reshape(1, D_MODEL), w_in_b[:, :n_cols], c4w, c4b.reshape(1, D_LRU), wg, gb, lam, h0)


def _ctx_state_kernel(a_ref, b_ref, o_ref):
    _, o_ref[...] = _recurrence(a_ref[...], b_ref[...], jnp.zeros((BATCH, D_LRU), F32), True)


def _ctx_state(a_b, b_b):
    n_rows = a_b.shape[0]
    return pl.pallas_call(
        _ctx_state_kernel,
        grid=(1,),
        in_specs=[pl.BlockSpec((n_rows, D_LRU), lambda i: (0, 0))] * 2,
        out_specs=pl.BlockSpec((BATCH, D_LRU), lambda i: (0, 0)),
        out_shape=jax.ShapeDtypeStruct((BATCH, D_LRU), F32),
        compiler_params=pltpu.CompilerParams(
            dimension_semantics=("arbitrary",), vmem_limit_bytes=VMEM_LIMIT_BYTES),
        name="ctx_state",
    )(a_b, b_b)


def _post_kernel(*refs, is_ctx, final_norm, natural_in, natural_out):
    refs = list(refs)
    x_ref, hf_ref, ab_ref, bb_ref, gt_ref, xc_ref, bg_ref, cg_ref, xcp_ref, cgp_ref, xcn_ref, cgn_ref = refs[:12]
    mod_ref, c3w_ref, gl_ref, gc_ref, wo_ref, n2g_ref, w1_ref, w2_ref, h0_ref = refs[12:21]
    refs = refs[21:]
    fg_ref = refs.pop(0) if final_norm else None
    o_ref, hend_ref, carry_s = refs[:3]
    slabs = refs[3] if (natural_in or natural_out) else None

    i = pl.program_id(0)
    n_i = pl.num_programs(0)
    j = n_i - 1 - i
    rows = TILE_ROWS
    m = mod_ref[...]
    w3 = c3w_ref[...]

    @pl.when(i == 0)
    def _():
        carry_s[...] = h0_ref[...]

    h_b, h_last = _recurrence(ab_ref[...], bb_ref[...], carry_s[...], True)
    carry_s[...] = h_last
    hend_ref[...] = h_last
    y_lru = _gelu_tanh(gt_ref[...]) * (hf_ref[...] + h_b)

    v = cg_ref[...] * xc_ref[...]
    zeros = jnp.zeros((BATCH, D_CONV), F32)
    if is_ctx:
        before = jnp.where(j > 0, cgp_ref[...] * xcp_ref[...], 0.0)
        after = jnp.where(j < n_i - 1, cgn_ref[...] * xcn_ref[...], 0.0)
        v_m = jnp.concatenate([before, v[:rows - BATCH]], axis=0)
        v_p = jnp.concatenate([v[BATCH:], after], axis=0)
        conv = w3[0:1] * v_m + w3[1:2] * v + w3[2:3] * v_p
    else:
        va = v[:, :HALF_CONV]
        va_m = jnp.concatenate([zeros[:, :HALF_CONV], va[:rows - BATCH]], axis=0)
        va_p = jnp.concatenate([va[BATCH:], zeros[:, :HALF_CONV]], axis=0)
        conv_row = w3[0:1, :HALF_CONV] * va_m + w3[1:2, :HALF_CONV] * va + w3[2:3, :HALF_CONV] * va_p
        vb_m = jnp.where(j > 0, cgp_ref[...] * xcp_ref[...], 0.0)
        vb_p = jnp.where(j < n_i - 1, cgn_ref[...] * xcn_ref[...], 0.0)
        conv_col = (w3[0:1, HALF_CONV:] * vb_m + w3[1:2, HALF_CONV:] * v[:, HALF_CONV:]
                    + w3[2:3, HALF_CONV:] * vb_p)
        conv = jnp.concatenate([conv_row, conv_col], axis=1)
    y_conv = bg_ref[...] * conv

    yl = _rms(y_lru, gl_ref[...]).astype(BF16)
    yc = _rms(y_conv, gc_ref[...]).astype(BF16)
    mix = _dot(yl, wo_ref[0:D_LRU, :]) + _dot(yc, wo_ref[D_LRU:, :])
    x_in = _to_time_major(x_ref, slabs) if natural_in else x_ref[...]
    x1 = x_in + _per_batch(mix, lambda t3: t3 * m[2][None])

    y2 = _rms(x1, n2g_ref[...])
    hb = _per_batch(y2, lambda y3: y3 * (1.0 + m[4])[None] + m[3][None]).astype(BF16)
    acc = jnp.zeros((rows, D_MODEL), F32)
    for k in range(D_FF // FF_CHUNK):
        t1 = jnp.maximum(_dot(hb, w1_ref[:, k * FF_CHUNK:(k + 1) * FF_CHUNK]), 0.0)
        acc = acc + _dot((t1 * t1).astype(BF16), w2_ref[k * FF_CHUNK:(k + 1) * FF_CHUNK, :])
    x2 = x1 + _per_batch(acc, lambda t3: t3 * m[5][None])
    if final_norm:
        x2 = _rms(x2, fg_ref[...])
    if natural_out:
        for s in range(slabs.shape[0]):
            slabs[s] = x2[:, s * LANES:(s + 1) * LANES]
        for b in range(BATCH):
            for s in range(slabs.shape[0]):
                o_ref[b, :, s * LANES:(s + 1) * LANES] = slabs[s, pl.ds(b, rows // BATCH, stride=BATCH), :]
    else:
        o_ref[...] = x2


def _post(x, hf, ab, bb, pieces, mod, c3w, g_lru, g_conv, w_out_b, n2g, w1_b, w2_b, h0, final_g, *, is_ctx,
          natural_out=False):
    natural_in = x.ndim == 3
    n_rows = x.shape[0] * x.shape[1] if natural_in else x.shape[0]
    n_tiles = n_rows // TILE_ROWS
    gt, xc, bg, cg = pieces
    final_norm = final_g is not None
    scratch = [pltpu.VMEM((BATCH, D_LRU), F32)]
    if natural_in or natural_out:
        scratch.append(pltpu.VMEM((D_MODEL // LANES, TILE_ROWS, LANES), F32))
    if natural_in:
        x_in_spec = pl.BlockSpec((BATCH, TILE_STEPS, D_MODEL), lambda i: (0, n_tiles - 1 - i, 0))
    else:
        x_in_spec = pl.BlockSpec((TILE_ROWS, D_MODEL), lambda i: (n_tiles - 1 - i, 0))
    if natural_out:
        x_out_spec = pl.BlockSpec((BATCH, TILE_STEPS, D_MODEL), lambda i: (0, n_tiles - 1 - i, 0))
        x_out_shape = jax.ShapeDtypeStruct((BATCH, n_rows // BATCH, D_MODEL), F32)
    else:
        x_out_spec = pl.BlockSpec((TILE_ROWS, D_MODEL), lambda i: (n_tiles - 1 - i, 0))
        x_out_shape = jax.ShapeDtypeStruct((n_rows, D_MODEL), F32)

    def rev(i):
        return n_tiles - 1 - i

    tile = lambda width: pl.BlockSpec((TILE_ROWS, width), lambda i: (rev(i), 0))
    if is_ctx:
        per = TILE_ROWS // BATCH
        last = n_rows // BATCH - 1
        prev_spec = pl.BlockSpec((BATCH, D_CONV), lambda i: (jnp.maximum(rev(i) * per - 1, 0), 0))
        next_spec = pl.BlockSpec((BATCH, D_CONV), lambda i: (jnp.minimum((rev(i) + 1) * per, last), 0))
    else:
        prev_spec = pl.BlockSpec((TILE_ROWS, HALF_CONV), lambda i: (jnp.maximum(rev(i) - 1, 0), 1))
        next_spec = pl.BlockSpec((TILE_ROWS, HALF_CONV), lambda i: (jnp.minimum(rev(i) + 1, n_tiles - 1), 1))
    args = [x, hf, ab, bb, gt, xc, bg, cg, xc, cg, xc, cg,
            mod, c3w, g_lru.reshape(1, D_LRU), g_conv.reshape(1, D_CONV), w_out_b, n2g.reshape(1, D_MODEL),
            w1_b, w2_b, h0]
    in_specs = ([x_in_spec] + [tile(D_LRU)] * 7 + [prev_spec, prev_spec, next_spec, next_spec]
                + [_const_spec(mod.shape), _const_spec(c3w.shape), _const_spec((1, D_LRU)),
                   _const_spec((1, D_CONV)), _const_spec(w_out_b.shape), _const_spec((1, D_MODEL)),
                   _const_spec(w1_b.shape), _const_spec(w2_b.shape), _const_spec((BATCH, D_LRU))])
    if final_norm:
        args.append(final_g.reshape(1, D_MODEL))
        in_specs.append(_const_spec((1, D_MODEL)))
    return pl.pallas_call(
        functools.partial(_post_kernel, is_ctx=is_ctx, final_norm=final_norm, natural_in=natural_in,
                          natural_out=natural_out),
        grid=(n_tiles,),
        in_specs=in_specs,
        out_specs=[x_out_spec, pl.BlockSpec((BATCH, D_LRU), lambda i: (0, 0))],
        out_shape=[x_out_shape, jax.ShapeDtypeStruct((BATCH, D_LRU), F32)],
        scratch_shapes=scratch,
        compiler_params=pltpu.CompilerParams(
            dimension_semantics=("arbitrary",), vmem_limit_bytes=VMEM_LIMIT_BYTES),
        name="post",
    )(*args)


def _pack_gates(w_a, w_x):
    per = GATE_BLOCK // LRU_HEAD_DIM
    eye = jnp.eye(per, dtype=F32)

    def blocks(w):
        w = w.reshape(N_DIR, LRU_HEADS // per, per, LRU_HEAD_DIM, LRU_HEAD_DIM)
        return jnp.einsum('dkhij,hg->dkhigj', w, eye).reshape(N_DIR, LRU_HEADS // per, GATE_BLOCK, GATE_BLOCK)

    return (0.5 * jnp.concatenate([blocks(w_a), blocks(w_x)], axis=-1).astype(BF16)).astype(BF16)


def kernel(x, c, ctx, c_ctx, ada_w, ada_b, norm1_g, norm2_g, w_in, conv4_w, conv4_b, gate_a_w, gate_a_b,
           gate_x_w, gate_x_b, rg_lambda, conv3_w, g_out_lru, g_out_conv, w_out, w_mlp1, w_mlp2, final_g):
    depth = ada_w.shape[0]
    bsz, seq, _ = x.shape
    assert bsz == BATCH and seq % TILE_STEPS == 0 and ctx.shape[1] % TILE_STEPS == 0
    cc = jnp.concatenate([c, c_ctx[None, :], jnp.zeros((MOD_ROWS - bsz - 1, D_MODEL), F32)], axis=0)
    mod = _ada(cc, ada_w, ada_b).reshape(depth, MOD_ROWS, N_MOD, D_MODEL)
    mod_lat = jnp.transpose(mod[:, :bsz], (0, 2, 1, 3))
    mod_ctx = jnp.broadcast_to(mod[:, bsz][:, :, None, :], mod_lat.shape)

    zero_state = jnp.zeros((BATCH, D_LRU), F32)

    for l in range(depth):
        last = l == depth - 1
        w_in_b = w_in[l].astype(BF16)
        w_out_b = w_out[l].astype(BF16)
        w1_b = w_mlp1[l].astype(BF16)
        w2_b = w_mlp2[l].astype(BF16)
        wg = _pack_gates(gate_a_w[l], gate_x_w[l])
        gb = 0.5 * jnp.stack([gate_a_b[l], gate_x_b[l]], axis=1)
        lru_params = (conv4_w[l], conv4_b[l], wg, gb, rg_lambda[l])
        post_params = (conv3_w[l], g_out_lru[l], g_out_conv[l], w_out_b, norm2_g[l], w1_b, w2_b)

        cx = _proj(ctx, mod_ctx[l], norm1_g[l], w_in_b, *lru_params, zero_state, n_pieces=0 if last else N_PIECES)
        lat = _proj(x, mod_lat[l], norm1_g[l], w_in_b, *lru_params, cx[3], n_pieces=N_PIECES)
        if last:
            h0_bwd = _ctx_state(cx[1], cx[2])
        else:
            ctx, h0_bwd = _post(ctx, cx[0], cx[1], cx[2], cx[4:], mod_ctx[l], *post_params, zero_state, None,
                                is_ctx=True)
        x, _ = _post(x, lat[0], lat[1], lat[2], lat[4:], mod_lat[l], *post_params, h0_bwd,
                     final_g if last else None, is_ctx=False, natural_out=last)
    return x
```

```python
import functools

import jax
import jax.numpy as jnp
from jax import lax
from jax.experimental import pallas as pl
from jax.experimental.pallas import tpu as pltpu

F32 = jnp.float32
BF16 = jnp.bfloat16

D_MODEL = 1024
D_LRU = 512
D_CONV = 512
HALF_CONV = D_CONV // 2
D_FF = 4 * D_MODEL
GRID_W = 64
LRU_HEADS = 8
LRU_HEAD_DIM = D_LRU // LRU_HEADS
N_MOD = 6
N_PIECES = 4
N_DIR = 2
RG_C = 8.0
EPS = 1e-6
LOG2_E = 1.4426950408889634

SUBLANES = 8
LANES = 128
BF16_SUBLANES = 2 * SUBLANES
BATCH = SUBLANES
GATE_BLOCK = 256
MOD_ROWS = 16
VMEM_LIMIT_BYTES = 56 * 1024 * 1024

ADA_COLS = 1536
TILE_STEPS = GRID_W
TILE_ROWS = TILE_STEPS * BATCH
FF_CHUNK = 1024


def _sigmoid(z):
    return 0.5 * (jnp.tanh(0.5 * z) + 1.0)


def _gelu_tanh(z):
    return 0.5 * z * (1.0 + jnp.tanh(0.7978845608028654 * (z + 0.044715 * (z * z * z))))


def _rms(x, g):
    return x * lax.rsqrt(jnp.mean(x * x, axis=-1, keepdims=True) + EPS) * g


def _dot(a, b):
    return jnp.dot(a, b, preferred_element_type=F32)


def _per_batch(x, fn):
    rows, d = x.shape
    return fn(x.reshape(rows // BATCH, BATCH, d)).reshape(rows, d)


def _const_spec(shape):
    nd = len(shape)
    return pl.BlockSpec(shape, lambda *_: (0,) * nd, pipeline_mode=pl.Buffered(1))


def _ada_kernel(cc_ref, w_ref, b_ref, win_ref, o_ref, winb_ref):
    cc = cc_ref[...]
    s = (cc * _sigmoid(cc)).astype(BF16)
    o_ref[0] = _dot(s, w_ref[0].astype(BF16)) + b_ref[0]
    winb_ref[...] = win_ref[0].astype(BF16)


def _ada(cc, ada_w, ada_b, w_in):
    depth, _, n = ada_w.shape
    n_j = n // ADA_COLS
    rows, cols = w_in.shape[1:]
    chunk = rows // (depth * n_j)
    assert chunk * depth * n_j == rows and chunk % BF16_SUBLANES == 0
    return pl.pallas_call(
        _ada_kernel,
        grid=(depth, n_j),
        in_specs=[
            pl.BlockSpec((MOD_ROWS, D_MODEL), lambda l, j: (0, 0)),
            pl.BlockSpec((1, D_MODEL, ADA_COLS), lambda l, j: (l, 0, j)),
            pl.BlockSpec((1, 1, ADA_COLS), lambda l, j: (l, 0, j)),
            pl.BlockSpec((1, chunk, cols), lambda l, j: (0, l * n_j + j, 0)),
        ],
        out_specs=[pl.BlockSpec((1, MOD_ROWS, ADA_COLS), lambda l, j: (l, 0, j)),
                   pl.BlockSpec((chunk, cols), lambda l, j: (l * n_j + j, 0))],
        out_shape=[jax.ShapeDtypeStruct((depth, MOD_ROWS, n), F32), jax.ShapeDtypeStruct((rows, cols), BF16)],
        compiler_params=pltpu.CompilerParams(
            dimension_semantics=("arbitrary", "arbitrary"), vmem_limit_bytes=VMEM_LIMIT_BYTES),
        name="ada",
    )(cc, ada_w, ada_b.reshape(depth, 1, n), w_in)


def _half_rate(lam):
    z = -lam
    return (-0.5 * RG_C * LOG2_E) * (jnp.maximum(z, 0.0) + jnp.log(1.0 + jnp.exp(-jnp.abs(z))))


def _lru_gates(vb, w_lo, w_hi, bias_a, bias_x):
    g_lo = _dot(vb[:, :GATE_BLOCK], w_lo)
    g_hi = _dot(vb[:, GATE_BLOCK:], w_hi)
    th_a = jnp.tanh(jnp.concatenate([g_lo[:, :GATE_BLOCK], g_hi[:, :GATE_BLOCK]], axis=1) + bias_a)
    th_x = jnp.tanh(jnp.concatenate([g_lo[:, GATE_BLOCK:], g_hi[:, GATE_BLOCK:]], axis=1) + bias_x)
    return th_a, th_x


def _lru_coeffs(half_v, th_a, th_x, half_rate):
    a = jnp.exp2(half_rate * th_a + half_rate)
    s = 1.0 - a * a
    root = jnp.where(s > 0.0, s * lax.rsqrt(s), 0.0)
    b = root * (half_v * th_x + half_v)
    return a, b


def _recurrence(a, b, h, reverse):
    steps = a.shape[0] // BATCH
    out = [None] * steps
    for t in (reversed(range(steps)) if reverse else range(steps)):
        sl = slice(t * BATCH, (t + 1) * BATCH)
        h = a[sl] * h + b[sl]
        out[t] = h
    return jnp.concatenate(out, axis=0), h


def _to_time_major(x_ref, xs):
    steps = x_ref.shape[1]
    for b in range(BATCH):
        for s in range(xs.shape[0]):
            xs[s, pl.ds(b, steps, stride=BATCH), :] = x_ref[b, :, s * LANES:(s + 1) * LANES]
    return jnp.concatenate([xs[s] for s in range(xs.shape[0])], axis=1)


def _cast_chunks(in_refs, out_refs):
    for w_ref, o_ref in zip(in_refs, out_refs):
        o_ref[...] = w_ref[0].astype(BF16)


def _cast_specs(casts, n_steps):
    in_specs, out_specs, out_shapes, operands = [], [], [], []
    for w, layer in casts:
        _, rows, cols = w.shape
        chunk = rows // n_steps
        assert chunk * n_steps == rows and chunk % BF16_SUBLANES == 0
        in_specs.append(pl.BlockSpec((1, chunk, cols), lambda i, layer=layer: (layer, i, 0)))
        out_specs.append(pl.BlockSpec((chunk, cols), lambda i: (i, 0)))
        out_shapes.append(jax.ShapeDtypeStruct((rows, cols), BF16))
        operands.append(w)
    return in_specs, out_specs, out_shapes, operands


def _proj_kernel(x_ref, xp_ref, xn_ref, mod_ref, g_ref, w_ref, c4w_ref, c4b_ref, wg_ref, gb_ref, lam_ref, h0_ref,
                 *rest, natural_in, n_pieces, n_casts):
    rest = list(rest)
    cast_in, rest = rest[:n_casts], rest[n_casts:]
    hf_ref, ab_ref, bb_ref, hend_ref = rest[:4]
    piece_refs, rest = rest[4:4 + n_pieces], rest[4 + n_pieces:]
    cast_out, rest = rest[:n_casts], rest[n_casts:]
    carry_s = rest[0]
    xs = rest[1] if natural_in else None
    _cast_chunks(cast_in, cast_out)
    j = pl.program_id(0)
    n_j = pl.num_programs(0)
    rows = TILE_ROWS
    m = mod_ref[...]
    if natural_in:
        last = xp_ref.shape[1] - 1
        xe = jnp.concatenate([xp_ref[:, last, :], _to_time_major(x_ref, xs), xn_ref[:, 0, :], xn_ref[:, 1, :]],
                             axis=0)
    else:
        xe = jnp.concatenate([xp_ref[...], x_ref[...], xn_ref[...]], axis=0)
    y = _rms(xe, g_ref[...])
    hb = _per_batch(y, lambda y3: y3 * (1.0 + m[1])[None] + m[0][None]).astype(BF16)
    hb_c = hb[BATCH:BATCH + rows]

    def piece(k):
        piece_refs[k][...] = _dot(hb_c, w_ref[:, (k + 1) * D_LRU:(k + 2) * D_LRU])

    u_ext = _dot(hb, w_ref[:, :D_LRU])
    u_ext = jnp.concatenate([jnp.where(j > 0, u_ext[:BATCH], 0.0), u_ext[BATCH:BATCH + rows],
                             jnp.where(j < n_j - 1, u_ext[BATCH + rows:], 0.0)], axis=0)
    for k in range(min(2, len(piece_refs))):
        piece(k)
    w4 = c4w_ref[...]
    v = c4b_ref[...]
    for k in range(4):
        v = v + w4[k:k + 1] * u_ext[k * BATCH:k * BATCH + rows]
    vb = v.astype(BF16)
    half_v = 0.5 * v
    gates = [_lru_gates(vb, wg_ref[d, 0], wg_ref[d, 1], gb_ref[d, 0:1, :], gb_ref[d, 1:2, :])
             for d in range(N_DIR)]
    for k in range(2, len(piece_refs)):
        piece(k)
    a_f, b_f = _lru_coeffs(half_v, gates[0][0], gates[0][1], _half_rate(lam_ref[0:1, :]))
    a_b, b_b = _lru_coeffs(half_v, gates[1][0], gates[1][1], _half_rate(lam_ref[1:2, :]))
    ab_ref[...] = a_b
    bb_ref[...] = b_b

    @pl.when(j == 0)
    def _():
        carry_s[...] = h0_ref[...]

    h_f, h_last = _recurrence(a_f, b_f, carry_s[...], False)
    hf_ref[...] = h_f
    carry_s[...] = h_last
    hend_ref[...] = h_last


def _proj(x, mod, g, w_in_b, c4w, c4b, wg, gb, lam, h0, *, n_pieces, casts=()):
    natural_in = x.ndim == 3
    n_rows = x.shape[0] * x.shape[1] if natural_in else x.shape[0]
    n_tiles = n_rows // TILE_ROWS
    n_cols = (1 + n_pieces) * D_LRU
    tile = pl.BlockSpec((TILE_ROWS, D_LRU), lambda j: (j, 0))
    stream = jax.ShapeDtypeStruct((n_rows, D_LRU), F32)
    state = jax.ShapeDtypeStruct((BATCH, D_LRU), F32)
    scratch = [pltpu.VMEM((BATCH, D_LRU), F32)]
    if natural_in:
        halo = SUBLANES
        per = TILE_STEPS // halo
        last_halo = x.shape[1] // halo - 1
        x_specs = [
            pl.BlockSpec((BATCH, TILE_STEPS, D_MODEL), lambda j: (0, j, 0)),
            pl.BlockSpec((BATCH, halo, D_MODEL), lambda j: (0, jnp.maximum(j * per - 1, 0), 0)),
            pl.BlockSpec((BATCH, halo, D_MODEL), lambda j: (0, jnp.minimum((j + 1) * per, last_halo), 0)),
        ]
        scratch.append(pltpu.VMEM((D_MODEL // LANES, TILE_ROWS, LANES), F32))
    else:
        per_prev = TILE_ROWS // BATCH
        per_next = TILE_ROWS // (2 * BATCH)
        last_next = n_rows // (2 * BATCH) - 1
        x_specs = [
            pl.BlockSpec((TILE_ROWS, D_MODEL), lambda j: (j, 0)),
            pl.BlockSpec((BATCH, D_MODEL), lambda j: (jnp.maximum(j * per_prev - 1, 0), 0)),
            pl.BlockSpec((2 * BATCH, D_MODEL), lambda j: (jnp.minimum((j + 1) * per_next, last_next), 0)),
        ]
    cast_in, cast_out, cast_shapes, cast_ops = _cast_specs(casts, n_tiles)
    return pl.pallas_call(
        functools.partial(_proj_kernel, natural_in=natural_in, n_pieces=n_pieces, n_casts=len(casts)),
        grid=(n_tiles,),
        in_specs=x_specs + [
            _const_spec(mod.shape),
            _const_spec((1, D_MODEL)),
            _const_spec((D_MODEL, n_cols)),
            _const_spec(c4w.shape),
            _const_spec((1, D_LRU)),
            _const_spec(wg.shape),
            _const_spec(gb.shape),
            _const_spec(lam.shape),
            _const_spec((BATCH, D_LRU)),
        ] + cast_in,
        out_specs=[tile, tile, tile, pl.BlockSpec((BATCH, D_LRU), lambda j: (0, 0))] + [tile] * n_pieces + cast_out,
        out_shape=[stream, stream, stream, state] + [stream] * n_pieces + cast_shapes,
        scratch_shapes=scratch,
        compiler_params=pltpu.CompilerParams(
            dimension_semantics=("arbitrary",), vmem_limit_bytes=VMEM_LIMIT_BYTES),
        name="proj",
    )(x, x, x, mod, g.reshape(1, D_MODEL), w_in_b, c4w, c4b.reshape(1, D_LRU), wg, gb, lam, h0, *cast_ops)


def _ctx_state_kernel(a_ref, b_ref, o_ref):
    _, o_ref[...] = _recurrence(a_ref[...], b_ref[...], jnp.zeros((BATCH, D_LRU), F32), True)


def _ctx_state(a_b, b_b):
    n_rows = a_b.shape[0]
    return pl.pallas_call(
        _ctx_state_kernel,
        grid=(1,),
        in_specs=[pl.BlockSpec((n_rows, D_LRU), lambda i: (0, 0))] * 2,
        out_specs=pl.BlockSpec((BATCH, D_LRU), lambda i: (0, 0)),
        out_shape=jax.ShapeDtypeStruct((BATCH, D_LRU), F32),
        compiler_params=pltpu.CompilerParams(
            dimension_semantics=("arbitrary",), vmem_limit_bytes=VMEM_LIMIT_BYTES),
        name="ctx_state",
    )(a_b, b_b)


def _post_kernel(*refs, is_ctx, final_norm, natural_in, natural_out, n_casts):
    refs = list(refs)
    x_ref, hf_ref, ab_ref, bb_ref, gt_ref, xc_ref, bg_ref, cg_ref, xcp_ref, cgp_ref, xcn_ref, cgn_ref = refs[:12]
    mod_ref, c3w_ref, gl_ref, gc_ref, wo_ref, n2g_ref, w1_ref, w2_ref, h0_ref = refs[12:21]
    refs = refs[21:]
    fg_ref = refs.pop(0) if final_norm else None
    cast_in, refs = refs[:n_casts], refs[n_casts:]
    o_ref, hend_ref = refs[:2]
    cast_out, refs = refs[2:2 + n_casts], refs[2 + n_casts:]
    carry_s = refs[0]
    slabs = refs[1] if (natural_in or natural_out) else None
    _cast_chunks(cast_in, cast_out)

    i = pl.program_id(0)
    n_i = pl.num_programs(0)
    j = n_i - 1 - i
    rows = TILE_ROWS
    m = mod_ref[...]
    w3 = c3w_ref[...]

    @pl.when(i == 0)
    def _():
        carry_s[...] = h0_ref[...]

    h_b, h_last = _recurrence(ab_ref[...], bb_ref[...], carry_s[...], True)
    carry_s[...] = h_last
    hend_ref[...] = h_last
    y_lru = _gelu_tanh(gt_ref[...]) * (hf_ref[...] + h_b)

    v = cg_ref[...] * xc_ref[...]
    zeros = jnp.zeros((BATCH, D_CONV), F32)
    if is_ctx:
        before = jnp.where(j > 0, cgp_ref[...] * xcp_ref[...], 0.0)
        after = jnp.where(j < n_i - 1, cgn_ref[...] * xcn_ref[...], 0.0)
        v_m = jnp.concatenate([before, v[:rows - BATCH]], axis=0)
        v_p = jnp.concatenate([v[BATCH:], after], axis=0)
        conv = w3[0:1] * v_m + w3[1:2] * v + w3[2:3] * v_p
    else:
        va = v[:, :HALF_CONV]
        va_m = jnp.concatenate([zeros[:, :HALF_CONV], va[:rows - BATCH]], axis=0)
        va_p = jnp.concatenate([va[BATCH:], zeros[:, :HALF_CONV]], axis=0)
        conv_row = w3[0:1, :HALF_CONV] * va_m + w3[1:2, :HALF_CONV] * va + w3[2:3, :HALF_CONV] * va_p
        vb_m = jnp.where(j > 0, cgp_ref[...] * xcp_ref[...], 0.0)
        vb_p = jnp.where(j < n_i - 1, cgn_ref[...] * xcn_ref[...], 0.0)
        conv_col = (w3[0:1, HALF_CONV:] * vb_m + w3[1:2, HALF_CONV:] * v[:, HALF_CONV:]
                    + w3[2:3, HALF_CONV:] * vb_p)
        conv = jnp.concatenate([conv_row, conv_col], axis=1)
    y_conv = bg_ref[...] * conv

    yl = _rms(y_lru, gl_ref[...]).astype(BF16)
    yc = _rms(y_conv, gc_ref[...]).astype(BF16)
    mix = _dot(yl, wo_ref[0:D_LRU, :]) + _dot(yc, wo_ref[D_LRU:, :])
    x_in = _to_time_major(x_ref, slabs) if natural_in else x_ref[...]
    x1 = x_in + _per_batch(mix, lambda t3: t3 * m[2][None])

    y2 = _rms(x1, n2g_ref[...])
    hb = _per_batch(y2, lambda y3: y3 * (1.0 + m[4])[None] + m[3][None]).astype(BF16)
    acc = jnp.zeros((rows, D_MODEL), F32)
    for k in range(D_FF // FF_CHUNK):
        t1 = jnp.maximum(_dot(hb, w1_ref[:, k * FF_CHUNK:(k + 1) * FF_CHUNK]), 0.0)
        acc = acc + _dot((t1 * t1).astype(BF16), w2_ref[k * FF_CHUNK:(k + 1) * FF_CHUNK, :])
    x2 = x1 + _per_batch(acc, lambda t3: t3 * m[5][None])
    if final_norm:
        x2 = _rms(x2, fg_ref[...])
    if natural_out:
        for s in range(slabs.shape[0]):
            slabs[s] = x2[:, s * LANES:(s + 1) * LANES]
        for b in range(BATCH):
            for s in range(slabs.shape[0]):
                o_ref[b, :, s * LANES:(s + 1) * LANES] = slabs[s, pl.ds(b, rows // BATCH, stride=BATCH), :]
    else:
        o_ref[...] = x2


def _post(x, hf, ab, bb, pieces, mod, c3w, g_lru, g_conv, w_out_b, n2g, w1_b, w2_b, h0, final_g, *, is_ctx,
          natural_out=False, casts=()):
    natural_in = x.ndim == 3
    n_rows = x.shape[0] * x.shape[1] if natural_in else x.shape[0]
    n_tiles = n_rows // TILE_ROWS
    gt, xc, bg, cg = pieces
    final_norm = final_g is not None
    scratch = [pltpu.VMEM((BATCH, D_LRU), F32)]
    if natural_in or natural_out:
        scratch.append(pltpu.VMEM((D_MODEL // LANES, TILE_ROWS, LANES), F32))
    if natural_in:
        x_in_spec = pl.BlockSpec((BATCH, TILE_STEPS, D_MODEL), lambda i: (0, n_tiles - 1 - i, 0))
    else:
        x_in_spec = pl.BlockSpec((TILE_ROWS, D_MODEL), lambda i: (n_tiles - 1 - i, 0))
    if natural_out:
        x_out_spec = pl.BlockSpec((BATCH, TILE_STEPS, D_MODEL), lambda i: (0, n_tiles - 1 - i, 0))
        x_out_shape = jax.ShapeDtypeStruct((BATCH, n_rows // BATCH, D_MODEL), F32)
    else:
        x_out_spec = pl.BlockSpec((TILE_ROWS, D_MODEL), lambda i: (n_tiles - 1 - i, 0))
        x_out_shape = jax.ShapeDtypeStruct((n_rows, D_MODEL), F32)

    def rev(i):
        return n_tiles - 1 - i

    tile = lambda width: pl.BlockSpec((TILE_ROWS, width), lambda i: (rev(i), 0))
    if is_ctx:
        per = TILE_ROWS // BATCH
        last = n_rows // BATCH - 1
        prev_spec = pl.BlockSpec((BATCH, D_CONV), lambda i: (jnp.maximum(rev(i) * per - 1, 0), 0))
        next_spec = pl.BlockSpec((BATCH, D_CONV), lambda i: (jnp.minimum((rev(i) + 1) * per, last), 0))
    else:
        prev_spec = pl.BlockSpec((TILE_ROWS, HALF_CONV), lambda i: (jnp.maximum(rev(i) - 1, 0), 1))
        next_spec = pl.BlockSpec((TILE_ROWS, HALF_CONV), lambda i: (jnp.minimum(rev(i) + 1, n_tiles - 1), 1))
    args = [x, hf, ab, bb, gt, xc, bg, cg, xc, cg, xc, cg,
            mod, c3w, g_lru.reshape(1, D_LRU), g_conv.reshape(1, D_CONV), w_out_b, n2g.reshape(1, D_MODEL),
            w1_b, w2_b, h0]
    in_specs = ([x_in_spec] + [tile(D_LRU)] * 7 + [prev_spec, prev_spec, next_spec, next_spec]
                + [_const_spec(mod.shape), _const_spec(c3w.shape), _const_spec((1, D_LRU)),
                   _const_spec((1, D_CONV)), _const_spec(w_out_b.shape), _const_spec((1, D_MODEL)),
                   _const_spec(w1_b.shape), _const_spec(w2_b.shape), _const_spec((BATCH, D_LRU))])
    if final_norm:
        args.append(final_g.reshape(1, D_MODEL))
        in_specs.append(_const_spec((1, D_MODEL)))
    cast_in, cast_out, cast_shapes, cast_ops = _cast_specs(casts, n_tiles)
    return pl.pallas_call(
        functools.partial(_post_kernel, is_ctx=is_ctx, final_norm=final_norm, natural_in=natural_in,
                          natural_out=natural_out, n_casts=len(casts)),
        grid=(n_tiles,),
        in_specs=in_specs + cast_in,
        out_specs=[x_out_spec, pl.BlockSpec((BATCH, D_LRU), lambda i: (0, 0))] + cast_out,
        out_shape=[x_out_shape, jax.ShapeDtypeStruct((BATCH, D_LRU), F32)] + cast_shapes,
        scratch_shapes=scratch,
        compiler_params=pltpu.CompilerParams(
            dimension_semantics=("arbitrary",), vmem_limit_bytes=VMEM_LIMIT_BYTES),
        name="post",
    )(*args, *cast_ops)


def _pack_gates(w_a, w_x):
    per = GATE_BLOCK // LRU_HEAD_DIM
    n_blk = LRU_HEADS // per
    eye = jnp.eye(per, dtype=F32)

    def blocks(w):
        w = w.reshape(w.shape[0], N_DIR, n_blk, per, LRU_HEAD_DIM, LRU_HEAD_DIM)
        return jnp.einsum('ldkhij,hg->ldkhigj', w, eye).reshape(w.shape[0], N_DIR, n_blk, GATE_BLOCK, GATE_BLOCK)

    return (0.5 * jnp.concatenate([blocks(w_a), blocks(w_x)], axis=-1).astype(BF16)).astype(BF16)


def kernel(x, c, ctx, c_ctx, ada_w, ada_b, norm1_g, norm2_g, w_in, conv4_w, conv4_b, gate_a_w, gate_a_b,
           gate_x_w, gate_x_b, rg_lambda, conv3_w, g_out_lru, g_out_conv, w_out, w_mlp1, w_mlp2, final_g):
    depth = ada_w.shape[0]
    bsz, seq, _ = x.shape
    assert bsz == BATCH and seq % TILE_STEPS == 0 and ctx.shape[1] % TILE_STEPS == 0
    cc = jnp.concatenate([c, c_ctx[None, :], jnp.zeros((MOD_ROWS - bsz - 1, D_MODEL), F32)], axis=0)
    mod, w_in_b = _ada(cc, ada_w, ada_b, w_in)
    mod = mod.reshape(depth, MOD_ROWS, N_MOD, D_MODEL)
    mod_lat = jnp.transpose(mod[:, :bsz], (0, 2, 1, 3))
    mod_ctx = jnp.broadcast_to(mod[:, bsz][:, :, None, :], mod_lat.shape)
    wg = _pack_gates(gate_a_w, gate_x_w)
    gb = 0.5 * jnp.stack([gate_a_b, gate_x_b], axis=2)
    zero_state = jnp.zeros((BATCH, D_LRU), F32)

    for l in range(depth):
        last = l == depth - 1
        lru_params = (conv4_w[l], conv4_b[l], wg[l], gb[l], rg_lambda[l])
        cx = _proj(ctx, mod_ctx[l], norm1_g[l], w_in_b, *lru_params, zero_state, n_pieces=0 if last else N_PIECES)
        lat = _proj(x, mod_lat[l], norm1_g[l], w_in_b, *lru_params, cx[3], n_pieces=N_PIECES,
                    casts=[(w_out, l), (w_mlp1, l), (w_mlp2, l)])
        w_out_b, w1_b, w2_b = lat[4 + N_PIECES:]
        post_params = (conv3_w[l], g_out_lru[l], g_out_conv[l], w_out_b, norm2_g[l], w1_b, w2_b)
        if last:
            h0_bwd = _ctx_state(cx[1], cx[2])
        else:
            ctx, h0_bwd = _post(ctx, cx[0], cx[1], cx[2], cx[4:], mod_ctx[l], *post_params, zero_state, None,
                                is_ctx=True)
        res = _post(x, lat[0], lat[1], lat[2], lat[4:4 + N_PIECES], mod_lat[l], *post_params, h0_bwd,
                    final_g if last else None, is_ctx=False, natural_out=last,
                    casts=[] if last else [(w_in, l + 1)])
        x = res[0]
        if not last:
            w_in_b = res[2]
    return x
```

```python
import functools

import jax
import jax.numpy as jnp
from jax import lax
from jax.experimental import pallas as pl
from jax.experimental.pallas import tpu as pltpu

F32 = jnp.float32
BF16 = jnp.bfloat16

D_MODEL = 1024
D_LRU = 512
D_CONV = 512
HALF_CONV = D_CONV // 2
D_FF = 4 * D_MODEL
GRID_W = 64
LRU_HEADS = 8
LRU_HEAD_DIM = D_LRU // LRU_HEADS
N_MOD = 6
SH1, SC1, G1, SH2, SC2, G2 = range(N_MOD)
N_PIECES = 4
N_DIR = 2
RG_C = 8.0
EPS = 1e-6
LOG2_E = 1.4426950408889634

SUBLANES = 8
LANES = 128
BF16_SUBLANES = 2 * SUBLANES
BATCH = SUBLANES
GATE_BLOCK = 256
MOD_ROWS = 16
VMEM_LIMIT_BYTES = 56 * 1024 * 1024

ADA_COLS = 1536
TILE_STEPS = GRID_W
TILE_ROWS = TILE_STEPS * BATCH
FF_CHUNK = 1024

PAR_ROWS = 16
ROW_NORM1, ROW_NORM2, ROW_FINAL, ROW_C4W01, ROW_C4W23, ROW_C4B_GLRU, ROW_GCONV_C3W0, ROW_C3W12 = range(8)
ROW_GATE_B = 8
ROW_LAMBDA = 10


def _sigmoid(z):
    return 0.5 * (jnp.tanh(0.5 * z) + 1.0)


def _gelu_tanh(z):
    return 0.5 * z * (1.0 + jnp.tanh(0.7978845608028654 * (z + 0.044715 * (z * z * z))))


def _unit_rms(x):
    return x * lax.rsqrt(jnp.mean(x * x, axis=-1, keepdims=True) + EPS)


def _rms(x, g):
    return _unit_rms(x) * g


def _dot(a, b):
    return jnp.dot(a, b, preferred_element_type=F32)


def _per_batch(x, fn):
    rows, d = x.shape
    return fn(x.reshape(rows // BATCH, BATCH, d)).reshape(rows, d)


def _norm_modulate(x, g, shift, scale):
    gain = g * (1.0 + scale)
    return _per_batch(_unit_rms(x), lambda y3: y3 * gain[None] + shift[None])


def _half(row, k):
    return row[:, k * D_LRU:(k + 1) * D_LRU]


def _mod_rows(mod_ref, k, is_ctx):
    cols = slice(k * D_MODEL, (k + 1) * D_MODEL)
    if is_ctx:
        return jnp.broadcast_to(mod_ref[0, BATCH:BATCH + 1, cols], (BATCH, D_MODEL))
    return mod_ref[0, 0:BATCH, cols]


def _const_spec(shape):
    nd = len(shape)
    return pl.BlockSpec(shape, lambda *_: (0,) * nd, pipeline_mode=pl.Buffered(1))


def _layer_spec(shape, layer):
    nd = len(shape)
    return pl.BlockSpec((1,) + tuple(shape[1:]), lambda *_: (layer,) + (0,) * (nd - 1), pipeline_mode=pl.Buffered(1))


def _ada_kernel(cc_ref, w_ref, b_ref, win_ref, o_ref, winb_ref):
    cc = cc_ref[...]
    s = (cc * _sigmoid(cc)).astype(BF16)
    o_ref[0] = _dot(s, w_ref[0].astype(BF16)) + b_ref[0]
    winb_ref[...] = win_ref[0].astype(BF16)


def _ada(cc, ada_w, ada_b, w_in):
    depth, _, n = ada_w.shape
    n_j = n // ADA_COLS
    rows, cols = w_in.shape[1:]
    chunk = rows // (depth * n_j)
    assert chunk * depth * n_j == rows and chunk % BF16_SUBLANES == 0
    return pl.pallas_call(
        _ada_kernel,
        grid=(depth, n_j),
        in_specs=[
            pl.BlockSpec((MOD_ROWS, D_MODEL), lambda l, j: (0, 0)),
            pl.BlockSpec((1, D_MODEL, ADA_COLS), lambda l, j: (l, 0, j)),
            pl.BlockSpec((1, 1, ADA_COLS), lambda l, j: (l, 0, j)),
            pl.BlockSpec((1, chunk, cols), lambda l, j: (0, l * n_j + j, 0)),
        ],
        out_specs=[pl.BlockSpec((1, MOD_ROWS, ADA_COLS), lambda l, j: (l, 0, j)),
                   pl.BlockSpec((chunk, cols), lambda l, j: (l * n_j + j, 0))],
        out_shape=[jax.ShapeDtypeStruct((depth, MOD_ROWS, n), F32), jax.ShapeDtypeStruct((rows, cols), BF16)],
        compiler_params=pltpu.CompilerParams(
            dimension_semantics=("arbitrary", "arbitrary"), vmem_limit_bytes=VMEM_LIMIT_BYTES),
        name="ada",
    )(cc, ada_w, ada_b.reshape(depth, 1, n), w_in)


def _half_rate(lam):
    z = -lam
    return (-0.5 * RG_C * LOG2_E) * (jnp.maximum(z, 0.0) + jnp.log(1.0 + jnp.exp(-jnp.abs(z))))


def _lru_gates(vb, w_lo, w_hi, bias_a, bias_x):
    g_lo = _dot(vb[:, :GATE_BLOCK], w_lo)
    g_hi = _dot(vb[:, GATE_BLOCK:], w_hi)
    th_a = jnp.tanh(jnp.concatenate([g_lo[:, :GATE_BLOCK], g_hi[:, :GATE_BLOCK]], axis=1) + bias_a)
    th_x = jnp.tanh(jnp.concatenate([g_lo[:, GATE_BLOCK:], g_hi[:, GATE_BLOCK:]], axis=1) + bias_x)
    return th_a, th_x


def _lru_coeffs(half_v, th_a, th_x, half_rate):
    a = jnp.exp2(half_rate * th_a + half_rate)
    s = 1.0 - a * a
    root = jnp.where(s > 0.0, s * lax.rsqrt(s), 0.0)
    b = root * (half_v * th_x + half_v)
    return a, b


def _recurrence(a, b, h, reverse):
    steps = a.shape[0] // BATCH
    out = [None] * steps
    for t in (reversed(range(steps)) if reverse else range(steps)):
        sl = slice(t * BATCH, (t + 1) * BATCH)
        h = a[sl] * h + b[sl]
        out[t] = h
    return jnp.concatenate(out, axis=0), h


def _to_time_major(x_ref, xs):
    steps = x_ref.shape[1]
    for b in range(BATCH):
        for s in range(xs.shape[0]):
            xs[s, pl.ds(b, steps, stride=BATCH), :] = x_ref[b, :, s * LANES:(s + 1) * LANES]
    return jnp.concatenate([xs[s] for s in range(xs.shape[0])], axis=1)


def _cast_chunks(in_refs, out_refs):
    for w_ref, o_ref in zip(in_refs, out_refs):
        o_ref[...] = w_ref[0].astype(BF16)


def _cast_specs(casts, n_steps):
    in_specs, out_specs, out_shapes, operands = [], [], [], []
    for w, layer in casts:
        _, rows, cols = w.shape
        chunk = rows // n_steps
        assert chunk * n_steps == rows and chunk % BF16_SUBLANES == 0
        in_specs.append(pl.BlockSpec((1, chunk, cols), lambda i, layer=layer: (layer, i, 0)))
        out_specs.append(pl.BlockSpec((chunk, cols), lambda i: (i, 0)))
        out_shapes.append(jax.ShapeDtypeStruct((rows, cols), BF16))
        operands.append(w)
    return in_specs, out_specs, out_shapes, operands


def _proj_kernel(x_ref, xp_ref, xn_ref, mod_ref, par_ref, w_ref, wg_ref, h0_ref, *rest, natural_in, is_ctx,
                 n_pieces, n_casts):
    rest = list(rest)
    cast_in, rest = rest[:n_casts], rest[n_casts:]
    hf_ref, ab_ref, bb_ref, hend_ref = rest[:4]
    piece_refs, rest = rest[4:4 + n_pieces], rest[4 + n_pieces:]
    cast_out, rest = rest[:n_casts], rest[n_casts:]
    carry_s = rest[0]
    xs = rest[1] if natural_in else None
    _cast_chunks(cast_in, cast_out)
    j = pl.program_id(0)
    n_j = pl.num_programs(0)
    rows = TILE_ROWS
    par = par_ref[0]
    if natural_in:
        last = xp_ref.shape[1] - 1
        xe = jnp.concatenate([xp_ref[:, last, :], _to_time_major(x_ref, xs), xn_ref[:, 0, :], xn_ref[:, 1, :]],
                             axis=0)
    else:
        xe = jnp.concatenate([xp_ref[...], x_ref[...], xn_ref[...]], axis=0)
    hb = _norm_modulate(xe, par[ROW_NORM1:ROW_NORM1 + 1], _mod_rows(mod_ref, SH1, is_ctx),
                        _mod_rows(mod_ref, SC1, is_ctx)).astype(BF16)
    hb_c = hb[BATCH:BATCH + rows]

    def piece(k):
        piece_refs[k][...] = _dot(hb_c, w_ref[:, (k + 1) * D_LRU:(k + 2) * D_LRU])

    u_ext = _dot(hb, w_ref[:, :D_LRU])
    u_ext = jnp.concatenate([jnp.where(j > 0, u_ext[:BATCH], 0.0), u_ext[BATCH:BATCH + rows],
                             jnp.where(j < n_j - 1, u_ext[BATCH + rows:], 0.0)], axis=0)
    for k in range(min(2, len(piece_refs))):
        piece(k)
    v = _half(par[ROW_C4B_GLRU:ROW_C4B_GLRU + 1], 0)
    for k in range(4):
        tap = _half(par[ROW_C4W01 + k // 2:ROW_C4W01 + k // 2 + 1], k % 2)
        v = v + tap * u_ext[k * BATCH:k * BATCH + rows]
    vb = v.astype(BF16)
    half_v = 0.5 * v
    gates = []
    for d in range(N_DIR):
        bias = par[ROW_GATE_B + d:ROW_GATE_B + d + 1]
        gates.append(_lru_gates(vb, wg_ref[0, d, 0], wg_ref[0, d, 1], _half(bias, 0), _half(bias, 1)))
    for k in range(2, len(piece_refs)):
        piece(k)
    lam = par[ROW_LAMBDA:ROW_LAMBDA + 1]
    a_f, b_f = _lru_coeffs(half_v, gates[0][0], gates[0][1], _half_rate(_half(lam, 0)))
    a_b, b_b = _lru_coeffs(half_v, gates[1][0], gates[1][1], _half_rate(_half(lam, 1)))
    ab_ref[...] = a_b
    bb_ref[...] = b_b

    @pl.when(j == 0)
    def _():
        carry_s[...] = h0_ref[...]

    h_f, h_last = _recurrence(a_f, b_f, carry_s[...], False)
    hf_ref[...] = h_f
    carry_s[...] = h_last
    hend_ref[...] = h_last


def _proj(x, mod, par, w_in_b, wg, h0, *, layer, is_ctx, n_pieces, casts=()):
    natural_in = x.ndim == 3
    n_rows = x.shape[0] * x.shape[1] if natural_in else x.shape[0]
    n_tiles = n_rows // TILE_ROWS
    n_cols = (1 + n_pieces) * D_LRU
    tile = pl.BlockSpec((TILE_ROWS, D_LRU), lambda j: (j, 0))
    stream = jax.ShapeDtypeStruct((n_rows, D_LRU), F32)
    state = jax.ShapeDtypeStruct((BATCH, D_LRU), F32)
    scratch = [pltpu.VMEM((BATCH, D_LRU), F32)]
    if natural_in:
        halo = SUBLANES
        per = TILE_STEPS // halo
        last_halo = x.shape[1] // halo - 1
        x_specs = [
            pl.BlockSpec((BATCH, TILE_STEPS, D_MODEL), lambda j: (0, j, 0)),
            pl.BlockSpec((BATCH, halo, D_MODEL), lambda j: (0, jnp.maximum(j * per - 1, 0), 0)),
            pl.BlockSpec((BATCH, halo, D_MODEL), lambda j: (0, jnp.minimum((j + 1) * per, last_halo), 0)),
        ]
        scratch.append(pltpu.VMEM((D_MODEL // LANES, TILE_ROWS, LANES), F32))
    else:
        per_prev = TILE_ROWS // BATCH
        per_next = TILE_ROWS // (2 * BATCH)
        last_next = n_rows // (2 * BATCH) - 1
        x_specs = [
            pl.BlockSpec((TILE_ROWS, D_MODEL), lambda j: (j, 0)),
            pl.BlockSpec((BATCH, D_MODEL), lambda j: (jnp.maximum(j * per_prev - 1, 0), 0)),
            pl.BlockSpec((2 * BATCH, D_MODEL), lambda j: (jnp.minimum((j + 1) * per_next, last_next), 0)),
        ]
    cast_in, cast_out, cast_shapes, cast_ops = _cast_specs(casts, n_tiles)
    return pl.pallas_call(
        functools.partial(_proj_kernel, natural_in=natural_in, is_ctx=is_ctx, n_pieces=n_pieces,
                          n_casts=len(casts)),
        grid=(n_tiles,),
        in_specs=x_specs + [
            _layer_spec(mod.shape, layer),
            _layer_spec(par.shape, layer),
            _const_spec((D_MODEL, n_cols)),
            _layer_spec(wg.shape, layer),
            _const_spec((BATCH, D_LRU)),
        ] + cast_in,
        out_specs=[tile, tile, tile, pl.BlockSpec((BATCH, D_LRU), lambda j: (0, 0))] + [tile] * n_pieces + cast_out,
        out_shape=[stream, stream, stream, state] + [stream] * n_pieces + cast_shapes,
        scratch_shapes=scratch,
        compiler_params=pltpu.CompilerParams(
            dimension_semantics=("arbitrary",), vmem_limit_bytes=VMEM_LIMIT_BYTES),
        name="proj",
    )(x, x, x, mod, par, w_in_b, wg, h0, *cast_ops)


def _ctx_state_kernel(a_ref, b_ref, o_ref):
    _, o_ref[...] = _recurrence(a_ref[...], b_ref[...], jnp.zeros((BATCH, D_LRU), F32), True)


def _ctx_state(a_b, b_b):
    n_rows = a_b.shape[0]
    return pl.pallas_call(
        _ctx_state_kernel,
        grid=(1,),
        in_specs=[pl.BlockSpec((n_rows, D_LRU), lambda i: (0, 0))] * 2,
        out_specs=pl.BlockSpec((BATCH, D_LRU), lambda i: (0, 0)),
        out_shape=jax.ShapeDtypeStruct((BATCH, D_LRU), F32),
        compiler_params=pltpu.CompilerParams(
            dimension_semantics=("arbitrary",), vmem_limit_bytes=VMEM_LIMIT_BYTES),
        name="ctx_state",
    )(a_b, b_b)


def _post_kernel(x_ref, hf_ref, ab_ref, bb_ref, gt_ref, xc_ref, bg_ref, cg_ref, xcp_ref, cgp_ref, xcn_ref, cgn_ref,
                 mod_ref, par_ref, wo_ref, w1_ref, w2_ref, h0_ref, *rest, is_ctx, final_norm, natural_in,
                 natural_out, n_casts):
    rest = list(rest)
    cast_in, rest = rest[:n_casts], rest[n_casts:]
    o_ref, hend_ref = rest[:2]
    cast_out, rest = rest[2:2 + n_casts], rest[2 + n_casts:]
    carry_s = rest[0]
    slabs = rest[1] if (natural_in or natural_out) else None
    _cast_chunks(cast_in, cast_out)

    i = pl.program_id(0)
    n_i = pl.num_programs(0)
    j = n_i - 1 - i
    rows = TILE_ROWS
    par = par_ref[0]
    g_lru = _half(par[ROW_C4B_GLRU:ROW_C4B_GLRU + 1], 1)
    g_conv = _half(par[ROW_GCONV_C3W0:ROW_GCONV_C3W0 + 1], 0)
    w3 = [_half(par[ROW_GCONV_C3W0:ROW_GCONV_C3W0 + 1], 1), _half(par[ROW_C3W12:ROW_C3W12 + 1], 0),
          _half(par[ROW_C3W12:ROW_C3W12 + 1], 1)]

    @pl.when(i == 0)
    def _():
        carry_s[...] = h0_ref[...]

    h_b, h_last = _recurrence(ab_ref[...], bb_ref[...], carry_s[...], True)
    carry_s[...] = h_last
    hend_ref[...] = h_last
    y_lru = _gelu_tanh(gt_ref[...]) * (hf_ref[...] + h_b)

    v = cg_ref[...] * xc_ref[...]
    zeros = jnp.zeros((BATCH, D_CONV), F32)
    if is_ctx:
        before = jnp.where(j > 0, cgp_ref[...] * xcp_ref[...], 0.0)
        after = jnp.where(j < n_i - 1, cgn_ref[...] * xcn_ref[...], 0.0)
        v_m = jnp.concatenate([before, v[:rows - BATCH]], axis=0)
        v_p = jnp.concatenate([v[BATCH:], after], axis=0)
        conv = w3[0] * v_m + w3[1] * v + w3[2] * v_p
    else:
        va = v[:, :HALF_CONV]
        va_m = jnp.concatenate([zeros[:, :HALF_CONV], va[:rows - BATCH]], axis=0)
        va_p = jnp.concatenate([va[BATCH:], zeros[:, :HALF_CONV]], axis=0)
        conv_row = w3[0][:, :HALF_CONV] * va_m + w3[1][:, :HALF_CONV] * va + w3[2][:, :HALF_CONV] * va_p
        vb_m = jnp.where(j > 0, cgp_ref[...] * xcp_ref[...], 0.0)
        vb_p = jnp.where(j < n_i - 1, cgn_ref[...] * xcn_ref[...], 0.0)
        conv_col = (w3[0][:, HALF_CONV:] * vb_m + w3[1][:, HALF_CONV:] * v[:, HALF_CONV:]
                    + w3[2][:, HALF_CONV:] * vb_p)
        conv = jnp.concatenate([conv_row, conv_col], axis=1)
    y_conv = bg_ref[...] * conv

    yl = _rms(y_lru, g_lru).astype(BF16)
    yc = _rms(y_conv, g_conv).astype(BF16)
    mix = _dot(yl, wo_ref[0:D_LRU, :]) + _dot(yc, wo_ref[D_LRU:, :])
    x_in = _to_time_major(x_ref, slabs) if natural_in else x_ref[...]
    gate1 = _mod_rows(mod_ref, G1, is_ctx)
    x1 = x_in + _per_batch(mix, lambda t3: t3 * gate1[None])

    hb = _norm_modulate(x1, par[ROW_NORM2:ROW_NORM2 + 1], _mod_rows(mod_ref, SH2, is_ctx),
                        _mod_rows(mod_ref, SC2, is_ctx)).astype(BF16)
    acc = jnp.zeros((rows, D_MODEL), F32)
    for k in range(D_FF // FF_CHUNK):
        t1 = jnp.maximum(_dot(hb, w1_ref[:, k * FF_CHUNK:(k + 1) * FF_CHUNK]), 0.0)
        acc = acc + _dot((t1 * t1).astype(BF16), w2_ref[k * FF_CHUNK:(k + 1) * FF_CHUNK, :])
    gate2 = _mod_rows(mod_ref, G2, is_ctx)
    x2 = x1 + _per_batch(acc, lambda t3: t3 * gate2[None])
    if final_norm:
        x2 = _rms(x2, par[ROW_FINAL:ROW_FINAL + 1])
    if natural_out:
        for s in range(slabs.shape[0]):
            slabs[s] = x2[:, s * LANES:(s + 1) * LANES]
        for b in range(BATCH):
            for s in range(slabs.shape[0]):
                o_ref[b, :, s * LANES:(s + 1) * LANES] = slabs[s, pl.ds(b, rows // BATCH, stride=BATCH), :]
    else:
        o_ref[...] = x2


def _post(x, hf, ab, bb, pieces, mod, par, w_out_b, w1_b, w2_b, h0, *, layer, is_ctx, final_norm=False,
          natural_out=False, casts=()):
    natural_in = x.ndim == 3
    n_rows = x.shape[0] * x.shape[1] if natural_in else x.shape[0]
    n_tiles = n_rows // TILE_ROWS
    gt, xc, bg, cg = pieces
    scratch = [pltpu.VMEM((BATCH, D_LRU), F32)]
    if natural_in or natural_out:
        scratch.append(pltpu.VMEM((D_MODEL // LANES, TILE_ROWS, LANES), F32))
    if natural_in:
        x_in_spec = pl.BlockSpec((BATCH, TILE_STEPS, D_MODEL), lambda i: (0, n_tiles - 1 - i, 0))
    else:
        x_in_spec = pl.BlockSpec((TILE_ROWS, D_MODEL), lambda i: (n_tiles - 1 - i, 0))
    if natural_out:
        x_out_spec = pl.BlockSpec((BATCH, TILE_STEPS, D_MODEL), lambda i: (0, n_tiles - 1 - i, 0))
        x_out_shape = jax.ShapeDtypeStruct((BATCH, n_rows // BATCH, D_MODEL), F32)
    else:
        x_out_spec = pl.BlockSpec((TILE_ROWS, D_MODEL), lambda i: (n_tiles - 1 - i, 0))
        x_out_shape = jax.ShapeDtypeStruct((n_rows, D_MODEL), F32)

    def rev(i):
        return n_tiles - 1 - i

    tile = lambda width: pl.BlockSpec((TILE_ROWS, width), lambda i: (rev(i), 0))
    if is_ctx:
        per = TILE_ROWS // BATCH
        last = n_rows // BATCH - 1
        prev_spec = pl.BlockSpec((BATCH, D_CONV), lambda i: (jnp.maximum(rev(i) * per - 1, 0), 0))
        next_spec = pl.BlockSpec((BATCH, D_CONV), lambda i: (jnp.minimum((rev(i) + 1) * per, last), 0))
    else:
        prev_spec = pl.BlockSpec((TILE_ROWS, HALF_CONV), lambda i: (jnp.maximum(rev(i) - 1, 0), 1))
        next_spec = pl.BlockSpec((TILE_ROWS, HALF_CONV), lambda i: (jnp.minimum(rev(i) + 1, n_tiles - 1), 1))
    args = [x, hf, ab, bb, gt, xc, bg, cg, xc, cg, xc, cg, mod, par, w_out_b, w1_b, w2_b, h0]
    in_specs = ([x_in_spec] + [tile(D_LRU)] * 7 + [prev_spec, prev_spec, next_spec, next_spec]
                + [_layer_spec(mod.shape, layer), _layer_spec(par.shape, layer), _const_spec(w_out_b.shape),
                   _const_spec(w1_b.shape), _const_spec(w2_b.shape), _const_spec((BATCH, D_LRU))])
    cast_in, cast_out, cast_shapes, cast_ops = _cast_specs(casts, n_tiles)
    return pl.pallas_call(
        functools.partial(_post_kernel, is_ctx=is_ctx, final_norm=final_norm, natural_in=natural_in,
                          natural_out=natural_out, n_casts=len(casts)),
        grid=(n_tiles,),
        in_specs=in_specs + cast_in,
        out_specs=[x_out_spec, pl.BlockSpec((BATCH, D_LRU), lambda i: (0, 0))] + cast_out,
        out_shape=[x_out_shape, jax.ShapeDtypeStruct((BATCH, D_LRU), F32)] + cast_shapes,
        scratch_shapes=scratch,
        compiler_params=pltpu.CompilerParams(
            dimension_semantics=("arbitrary",), vmem_limit_bytes=VMEM_LIMIT_BYTES),
        name="post",
    )(*args, *cast_ops)


def _pack_gates(w_a, w_x):
    per = GATE_BLOCK // LRU_HEAD_DIM
    n_blk = LRU_HEADS // per
    eye = jnp.eye(per, dtype=F32)

    def blocks(w):
        w = w.reshape(w.shape[0], N_DIR, n_blk, per, LRU_HEAD_DIM, LRU_HEAD_DIM)
        return jnp.einsum('ldkhij,hg->ldkhigj', w, eye).reshape(w.shape[0], N_DIR, n_blk, GATE_BLOCK, GATE_BLOCK)

    return (0.5 * jnp.concatenate([blocks(w_a), blocks(w_x)], axis=-1).astype(BF16)).astype(BF16)


def _pack_small(norm1_g, norm2_g, final_g, conv4_w, conv4_b, g_out_lru, g_out_conv, conv3_w, gate_a_b, gate_x_b,
                rg_lambda):
    depth = norm1_g.shape[0]
    pair = lambda a, b: jnp.concatenate([a, b], axis=-1)
    rows = [None] * (ROW_LAMBDA + 1)
    rows[ROW_NORM1] = norm1_g
    rows[ROW_NORM2] = norm2_g
    rows[ROW_FINAL] = jnp.broadcast_to(final_g[None, :], (depth, D_MODEL))
    rows[ROW_C4W01] = pair(conv4_w[:, 0], conv4_w[:, 1])
    rows[ROW_C4W23] = pair(conv4_w[:, 2], conv4_w[:, 3])
    rows[ROW_C4B_GLRU] = pair(conv4_b, g_out_lru)
    rows[ROW_GCONV_C3W0] = pair(g_out_conv, conv3_w[:, 0])
    rows[ROW_C3W12] = pair(conv3_w[:, 1], conv3_w[:, 2])
    for d in range(N_DIR):
        rows[ROW_GATE_B + d] = 0.5 * pair(gate_a_b[:, d], gate_x_b[:, d])
    rows[ROW_LAMBDA] = pair(rg_lambda[:, 0], rg_lambda[:, 1])
    rows += [jnp.zeros((depth, D_MODEL), F32)] * (PAR_ROWS - len(rows))
    return jnp.stack(rows, axis=1)


def kernel(x, c, ctx, c_ctx, ada_w, ada_b, norm1_g, norm2_g, w_in, conv4_w, conv4_b, gate_a_w, gate_a_b,
           gate_x_w, gate_x_b, rg_lambda, conv3_w, g_out_lru, g_out_conv, w_out, w_mlp1, w_mlp2, final_g):
    depth = ada_w.shape[0]
    bsz, seq, _ = x.shape
    assert bsz == BATCH and seq % TILE_STEPS == 0 and ctx.shape[1] % TILE_STEPS == 0
    cc = jnp.concatenate([c, c_ctx[None, :], jnp.zeros((MOD_ROWS - bsz - 1, D_MODEL), F32)], axis=0)
    mod, w_in_b = _ada(cc, ada_w, ada_b, w_in)
    par = _pack_small(norm1_g, norm2_g, final_g, conv4_w, conv4_b, g_out_lru, g_out_conv, conv3_w, gate_a_b,
                      gate_x_b, rg_lambda)
    wg = _pack_gates(gate_a_w, gate_x_w)
    zero_state = jnp.zeros((BATCH, D_LRU), F32)

    for l in range(depth):
        last = l == depth - 1
        cx = _proj(ctx, mod, par, w_in_b, wg, zero_state, layer=l, is_ctx=True, n_pieces=0 if last else N_PIECES)
        lat = _proj(x, mod, par, w_in_b, wg, cx[3], layer=l, is_ctx=False, n_pieces=N_PIECES,
                    casts=[(w_out, l), (w_mlp1, l), (w_mlp2, l)])
        w_out_b, w1_b, w2_b = lat[4 + N_PIECES:]
        if last:
            h0_bwd = _ctx_state(cx[1], cx[2])
        else:
            ctx, h0_bwd = _post(ctx, cx[0], cx[1], cx[2], cx[4:], mod, par, w_out_b, w1_b, w2_b, zero_state,
                                layer=l, is_ctx=True)
        res = _post(x, lat[0], lat[1], lat[2], lat[4:4 + N_PIECES], mod, par, w_out_b, w1_b, w2_b, h0_bwd,
                    layer=l, is_ctx=False, final_norm=last, natural_out=last,
                    casts=[] if last else [(w_in, l + 1)])
        x = res[0]
        if not last:
            w_in_b = res[2]
    return x
```

```python
import functools

import jax
import jax.numpy as jnp
from jax import lax
from jax.experimental import pallas as pl
from jax.experimental.pallas import tpu as pltpu

F32 = jnp.float32
BF16 = jnp.bfloat16

D_MODEL = 1024
D_LRU = 512
D_CONV = 512
HALF_CONV = D_CONV // 2
D_FF = 4 * D_MODEL
GRID_W = 64
LRU_HEADS = 8
LRU_HEAD_DIM = D_LRU // LRU_HEADS
N_MOD = 6
SH1, SC1, G1, SH2, SC2, G2 = range(N_MOD)
COL_LRU, COL_GATE, COL_X, COL_B, COL_C = range(5)
N_PIECES = 3
N_DIR = 2
RG_C = 8.0
EPS = 1e-6
LOG2_E = 1.4426950408889634

SUBLANES = 8
LANES = 128
BF16_SUBLANES = 2 * SUBLANES
BATCH = SUBLANES
GATE_BLOCK = 256
MOD_ROWS = 16
VMEM_LIMIT_BYTES = 56 * 1024 * 1024

ADA_COLS = 1536
TILE_STEPS = GRID_W
TILE_ROWS = TILE_STEPS * BATCH
FF_CHUNK = 1024

PAR_ROWS = 16
ROW_NORM1, ROW_NORM2, ROW_FINAL, ROW_C4W01, ROW_C4W23, ROW_C4B_GLRU, ROW_GCONV_C3W0, ROW_C3W12 = range(8)
ROW_GATE_B = 8
ROW_LAMBDA = 10


def _sigmoid(z):
    return 0.5 * (jnp.tanh(0.5 * z) + 1.0)


def _gelu_tanh(z):
    return 0.5 * z * (1.0 + jnp.tanh(0.7978845608028654 * (z + 0.044715 * (z * z * z))))


def _unit_rms(x):
    return x * lax.rsqrt(jnp.mean(x * x, axis=-1, keepdims=True) + EPS)


def _rms(x, g):
    return _unit_rms(x) * g


def _dot(a, b):
    return jnp.dot(a, b, preferred_element_type=F32)


def _per_batch(x, fn):
    rows, d = x.shape
    return fn(x.reshape(rows // BATCH, BATCH, d)).reshape(rows, d)


def _norm_modulate(x, g, shift, scale):
    gain = g * (1.0 + scale)
    return _per_batch(_unit_rms(x), lambda y3: y3 * gain[None] + shift[None])


def _half(row, k):
    return row[:, k * D_LRU:(k + 1) * D_LRU]


def _mod_rows(mod_ref, k, is_ctx):
    cols = slice(k * D_MODEL, (k + 1) * D_MODEL)
    if is_ctx:
        return jnp.broadcast_to(mod_ref[0, BATCH:BATCH + 1, cols], (BATCH, D_MODEL))
    return mod_ref[0, 0:BATCH, cols]


def _const_spec(shape):
    nd = len(shape)
    return pl.BlockSpec(shape, lambda *_: (0,) * nd, pipeline_mode=pl.Buffered(1))


def _layer_spec(shape, layer):
    nd = len(shape)
    return pl.BlockSpec((1,) + tuple(shape[1:]), lambda *_: (layer,) + (0,) * (nd - 1), pipeline_mode=pl.Buffered(1))


def _ada_kernel(cc_ref, w_ref, b_ref, win_ref, o_ref, winb_ref):
    cc = cc_ref[...]
    s = (cc * _sigmoid(cc)).astype(BF16)
    o_ref[0] = _dot(s, w_ref[0].astype(BF16)) + b_ref[0]
    winb_ref[...] = win_ref[0].astype(BF16)


def _ada(cc, ada_w, ada_b, w_in):
    depth, _, n = ada_w.shape
    n_j = n // ADA_COLS
    rows, cols = w_in.shape[1:]
    chunk = rows // (depth * n_j)
    assert chunk * depth * n_j == rows and chunk % BF16_SUBLANES == 0
    return pl.pallas_call(
        _ada_kernel,
        grid=(depth, n_j),
        in_specs=[
            pl.BlockSpec((MOD_ROWS, D_MODEL), lambda l, j: (0, 0)),
            pl.BlockSpec((1, D_MODEL, ADA_COLS), lambda l, j: (l, 0, j)),
            pl.BlockSpec((1, 1, ADA_COLS), lambda l, j: (l, 0, j)),
            pl.BlockSpec((1, chunk, cols), lambda l, j: (0, l * n_j + j, 0)),
        ],
        out_specs=[pl.BlockSpec((1, MOD_ROWS, ADA_COLS), lambda l, j: (l, 0, j)),
                   pl.BlockSpec((chunk, cols), lambda l, j: (l * n_j + j, 0))],
        out_shape=[jax.ShapeDtypeStruct((depth, MOD_ROWS, n), F32), jax.ShapeDtypeStruct((rows, cols), BF16)],
        compiler_params=pltpu.CompilerParams(
            dimension_semantics=("arbitrary", "arbitrary"), vmem_limit_bytes=VMEM_LIMIT_BYTES),
        name="ada",
    )(cc, ada_w, ada_b.reshape(depth, 1, n), w_in)


def _half_rate(lam):
    z = -lam
    return (-0.5 * RG_C * LOG2_E) * (jnp.maximum(z, 0.0) + jnp.log(1.0 + jnp.exp(-jnp.abs(z))))


def _lru_gates(vb, w_lo, w_hi, bias_a, bias_x):
    g_lo = _dot(vb[:, :GATE_BLOCK], w_lo)
    g_hi = _dot(vb[:, GATE_BLOCK:], w_hi)
    th_a = jnp.tanh(jnp.concatenate([g_lo[:, :GATE_BLOCK], g_hi[:, :GATE_BLOCK]], axis=1) + bias_a)
    th_x = jnp.tanh(jnp.concatenate([g_lo[:, GATE_BLOCK:], g_hi[:, GATE_BLOCK:]], axis=1) + bias_x)
    return th_a, th_x


def _lru_coeffs(half_v, th_a, th_x, half_rate):
    a = jnp.exp2(half_rate * th_a + half_rate)
    s = 1.0 - a * a
    root = jnp.where(s > 0.0, s * lax.rsqrt(s), 0.0)
    b = root * (half_v * th_x + half_v)
    return a, b


def _recurrence(a, b, h, reverse):
    steps = a.shape[0] // BATCH
    out = [None] * steps
    for t in (reversed(range(steps)) if reverse else range(steps)):
        sl = slice(t * BATCH, (t + 1) * BATCH)
        h = a[sl] * h + b[sl]
        out[t] = h
    return jnp.concatenate(out, axis=0), h


def _to_time_major(x_ref, xs):
    steps = x_ref.shape[1]
    for b in range(BATCH):
        for s in range(xs.shape[0]):
            xs[s, pl.ds(b, steps, stride=BATCH), :] = x_ref[b, :, s * LANES:(s + 1) * LANES]
    return jnp.concatenate([xs[s] for s in range(xs.shape[0])], axis=1)


def _cast_chunks(in_refs, out_refs):
    for w_ref, o_ref in zip(in_refs, out_refs):
        o_ref[...] = w_ref[0].astype(BF16)


def _cast_specs(casts, n_steps):
    in_specs, out_specs, out_shapes, operands = [], [], [], []
    for w, layer in casts:
        _, rows, cols = w.shape
        chunk = rows // n_steps
        assert chunk * n_steps == rows and chunk % BF16_SUBLANES == 0
        in_specs.append(pl.BlockSpec((1, chunk, cols), lambda i, layer=layer: (layer, i, 0)))
        out_specs.append(pl.BlockSpec((chunk, cols), lambda i: (i, 0)))
        out_shapes.append(jax.ShapeDtypeStruct((rows, cols), BF16))
        operands.append(w)
    return in_specs, out_specs, out_shapes, operands


def _proj_kernel(x_ref, xp_ref, xn_ref, mod_ref, par_ref, w_ref, wg_ref, h0_ref, *rest, natural_in, is_ctx,
                 n_pieces, n_casts):
    rest = list(rest)
    cast_in, rest = rest[:n_casts], rest[n_casts:]
    hf_ref, ab_ref, bb_ref, hend_ref = rest[:4]
    piece_refs, rest = rest[4:4 + n_pieces], rest[4 + n_pieces:]
    cast_out, rest = rest[:n_casts], rest[n_casts:]
    carry_s = rest[0]
    xs = rest[1] if natural_in else None
    _cast_chunks(cast_in, cast_out)
    j = pl.program_id(0)
    n_j = pl.num_programs(0)
    rows = TILE_ROWS
    par = par_ref[0]
    if natural_in:
        last = xp_ref.shape[1] - 1
        xe = jnp.concatenate([xp_ref[:, last, :], _to_time_major(x_ref, xs), xn_ref[:, 0, :], xn_ref[:, 1, :]],
                             axis=0)
    else:
        xe = jnp.concatenate([xp_ref[...], x_ref[...], xn_ref[...]], axis=0)
    hb = _norm_modulate(xe, par[ROW_NORM1:ROW_NORM1 + 1], _mod_rows(mod_ref, SH1, is_ctx),
                        _mod_rows(mod_ref, SC1, is_ctx)).astype(BF16)
    hb_c = hb[BATCH:BATCH + rows]

    def group(k):
        return _dot(hb_c, w_ref[:, k * D_LRU:(k + 1) * D_LRU])

    u_ext = _dot(hb, w_ref[:, :D_LRU])
    u_ext = jnp.concatenate([jnp.where(j > 0, u_ext[:BATCH], 0.0), u_ext[BATCH:BATCH + rows],
                             jnp.where(j < n_j - 1, u_ext[BATCH + rows:], 0.0)], axis=0)
    if piece_refs:
        gt_ref, v_ref, bg_ref = piece_refs
        gt_ref[...] = group(COL_GATE)
        bg_ref[...] = group(COL_B)
    v = _half(par[ROW_C4B_GLRU:ROW_C4B_GLRU + 1], 0)
    for k in range(4):
        tap = _half(par[ROW_C4W01 + k // 2:ROW_C4W01 + k // 2 + 1], k % 2)
        v = v + tap * u_ext[k * BATCH:k * BATCH + rows]
    vb = v.astype(BF16)
    half_v = 0.5 * v
    gates = []
    for d in range(N_DIR):
        bias = par[ROW_GATE_B + d:ROW_GATE_B + d + 1]
        gates.append(_lru_gates(vb, wg_ref[0, d, 0], wg_ref[0, d, 1], _half(bias, 0), _half(bias, 1)))
    if piece_refs:
        v_ref[...] = group(COL_C) * group(COL_X)
    lam = par[ROW_LAMBDA:ROW_LAMBDA + 1]
    a_f, b_f = _lru_coeffs(half_v, gates[0][0], gates[0][1], _half_rate(_half(lam, 0)))
    a_b, b_b = _lru_coeffs(half_v, gates[1][0], gates[1][1], _half_rate(_half(lam, 1)))
    ab_ref[...] = a_b
    bb_ref[...] = b_b

    @pl.when(j == 0)
    def _():
        carry_s[...] = h0_ref[...]

    h_f, h_last = _recurrence(a_f, b_f, carry_s[...], False)
    hf_ref[...] = h_f
    carry_s[...] = h_last
    hend_ref[...] = h_last


def _proj(x, mod, par, w_in_b, wg, h0, *, layer, is_ctx, n_pieces, casts=()):
    natural_in = x.ndim == 3
    n_rows = x.shape[0] * x.shape[1] if natural_in else x.shape[0]
    n_tiles = n_rows // TILE_ROWS
    n_cols = (COL_C + 1) * D_LRU if n_pieces else D_LRU
    tile = pl.BlockSpec((TILE_ROWS, D_LRU), lambda j: (j, 0))
    stream = jax.ShapeDtypeStruct((n_rows, D_LRU), F32)
    state = jax.ShapeDtypeStruct((BATCH, D_LRU), F32)
    scratch = [pltpu.VMEM((BATCH, D_LRU), F32)]
    if natural_in:
        halo = SUBLANES
        per = TILE_STEPS // halo
        last_halo = x.shape[1] // halo - 1
        x_specs = [
            pl.BlockSpec((BATCH, TILE_STEPS, D_MODEL), lambda j: (0, j, 0)),
            pl.BlockSpec((BATCH, halo, D_MODEL), lambda j: (0, jnp.maximum(j * per - 1, 0), 0)),
            pl.BlockSpec((BATCH, halo, D_MODEL), lambda j: (0, jnp.minimum((j + 1) * per, last_halo), 0)),
        ]
        scratch.append(pltpu.VMEM((D_MODEL // LANES, TILE_ROWS, LANES), F32))
    else:
        per_prev = TILE_ROWS // BATCH
        per_next = TILE_ROWS // (2 * BATCH)
        last_next = n_rows // (2 * BATCH) - 1
        x_specs = [
            pl.BlockSpec((TILE_ROWS, D_MODEL), lambda j: (j, 0)),
            pl.BlockSpec((BATCH, D_MODEL), lambda j: (jnp.maximum(j * per_prev - 1, 0), 0)),
            pl.BlockSpec((2 * BATCH, D_MODEL), lambda j: (jnp.minimum((j + 1) * per_next, last_next), 0)),
        ]
    cast_in, cast_out, cast_shapes, cast_ops = _cast_specs(casts, n_tiles)
    return pl.pallas_call(
        functools.partial(_proj_kernel, natural_in=natural_in, is_ctx=is_ctx, n_pieces=n_pieces,
                          n_casts=len(casts)),
        grid=(n_tiles,),
        in_specs=x_specs + [
            _layer_spec(mod.shape, layer),
            _layer_spec(par.shape, layer),
            _const_spec((D_MODEL, n_cols)),
            _layer_spec(wg.shape, layer),
            _const_spec((BATCH, D_LRU)),
        ] + cast_in,
        out_specs=[tile, tile, tile, pl.BlockSpec((BATCH, D_LRU), lambda j: (0, 0))] + [tile] * n_pieces + cast_out,
        out_shape=[stream, stream, stream, state] + [stream] * n_pieces + cast_shapes,
        scratch_shapes=scratch,
        compiler_params=pltpu.CompilerParams(
            dimension_semantics=("arbitrary",), vmem_limit_bytes=VMEM_LIMIT_BYTES),
        name="proj",
    )(x, x, x, mod, par, w_in_b, wg, h0, *cast_ops)


def _ctx_state_kernel(a_ref, b_ref, o_ref):
    _, o_ref[...] = _recurrence(a_ref[...], b_ref[...], jnp.zeros((BATCH, D_LRU), F32), True)


def _ctx_state(a_b, b_b):
    n_rows = a_b.shape[0]
    return pl.pallas_call(
        _ctx_state_kernel,
        grid=(1,),
        in_specs=[pl.BlockSpec((n_rows, D_LRU), lambda i: (0, 0))] * 2,
        out_specs=pl.BlockSpec((BATCH, D_LRU), lambda i: (0, 0)),
        out_shape=jax.ShapeDtypeStruct((BATCH, D_LRU), F32),
        compiler_params=pltpu.CompilerParams(
            dimension_semantics=("arbitrary",), vmem_limit_bytes=VMEM_LIMIT_BYTES),
        name="ctx_state",
    )(a_b, b_b)


def _post_kernel(x_ref, hf_ref, ab_ref, bb_ref, gt_ref, v_ref, bg_ref, vp_ref, vn_ref, mod_ref, par_ref, wo_ref,
                 w1_ref, w2_ref, h0_ref, *rest, is_ctx, final_norm, natural_in, natural_out, n_casts):
    rest = list(rest)
    cast_in, rest = rest[:n_casts], rest[n_casts:]
    o_ref, hend_ref = rest[:2]
    cast_out, rest = rest[2:2 + n_casts], rest[2 + n_casts:]
    carry_s = rest[0]
    slabs = rest[1] if (natural_in or natural_out) else None
    _cast_chunks(cast_in, cast_out)

    i = pl.program_id(0)
    n_i = pl.num_programs(0)
    j = n_i - 1 - i
    rows = TILE_ROWS
    par = par_ref[0]
    g_lru = _half(par[ROW_C4B_GLRU:ROW_C4B_GLRU + 1], 1)
    g_conv = _half(par[ROW_GCONV_C3W0:ROW_GCONV_C3W0 + 1], 0)
    w3 = [_half(par[ROW_GCONV_C3W0:ROW_GCONV_C3W0 + 1], 1), _half(par[ROW_C3W12:ROW_C3W12 + 1], 0),
          _half(par[ROW_C3W12:ROW_C3W12 + 1], 1)]

    @pl.when(i == 0)
    def _():
        carry_s[...] = h0_ref[...]

    h_b, h_last = _recurrence(ab_ref[...], bb_ref[...], carry_s[...], True)
    carry_s[...] = h_last
    hend_ref[...] = h_last
    y_lru = _gelu_tanh(gt_ref[...]) * (hf_ref[...] + h_b)

    v = v_ref[...]
    zeros = jnp.zeros((BATCH, D_CONV), F32)
    if is_ctx:
        before = jnp.where(j > 0, vp_ref[...], 0.0)
        after = jnp.where(j < n_i - 1, vn_ref[...], 0.0)
        v_m = jnp.concatenate([before, v[:rows - BATCH]], axis=0)
        v_p = jnp.concatenate([v[BATCH:], after], axis=0)
        conv = w3[0] * v_m + w3[1] * v + w3[2] * v_p
    else:
        va = v[:, :HALF_CONV]
        va_m = jnp.concatenate([zeros[:, :HALF_CONV], va[:rows - BATCH]], axis=0)
        va_p = jnp.concatenate([va[BATCH:], zeros[:, :HALF_CONV]], axis=0)
        conv_row = w3[0][:, :HALF_CONV] * va_m + w3[1][:, :HALF_CONV] * va + w3[2][:, :HALF_CONV] * va_p
        vb_m = jnp.where(j > 0, vp_ref[...], 0.0)
        vb_p = jnp.where(j < n_i - 1, vn_ref[...], 0.0)
        conv_col = (w3[0][:, HALF_CONV:] * vb_m + w3[1][:, HALF_CONV:] * v[:, HALF_CONV:]
                    + w3[2][:, HALF_CONV:] * vb_p)
        conv = jnp.concatenate([conv_row, conv_col], axis=1)
    y_conv = bg_ref[...] * conv

    yl = _rms(y_lru, g_lru).astype(BF16)
    yc = _rms(y_conv, g_conv).astype(BF16)
    mix = _dot(yl, wo_ref[0:D_LRU, :]) + _dot(yc, wo_ref[D_LRU:, :])
    x_in = _to_time_major(x_ref, slabs) if natural_in else x_ref[...]
    gate1 = _mod_rows(mod_ref, G1, is_ctx)
    x1 = x_in + _per_batch(mix, lambda t3: t3 * gate1[None])

    hb = _norm_modulate(x1, par[ROW_NORM2:ROW_NORM2 + 1], _mod_rows(mod_ref, SH2, is_ctx),
                        _mod_rows(mod_ref, SC2, is_ctx)).astype(BF16)
    acc = jnp.zeros((rows, D_MODEL), F32)
    for k in range(D_FF // FF_CHUNK):
        t1 = jnp.maximum(_dot(hb, w1_ref[:, k * FF_CHUNK:(k + 1) * FF_CHUNK]), 0.0)
        acc = acc + _dot((t1 * t1).astype(BF16), w2_ref[k * FF_CHUNK:(k + 1) * FF_CHUNK, :])
    gate2 = _mod_rows(mod_ref, G2, is_ctx)
    x2 = x1 + _per_batch(acc, lambda t3: t3 * gate2[None])
    if final_norm:
        x2 = _rms(x2, par[ROW_FINAL:ROW_FINAL + 1])
    if natural_out:
        for s in range(slabs.shape[0]):
            slabs[s] = x2[:, s * LANES:(s + 1) * LANES]
        for b in range(BATCH):
            for s in range(slabs.shape[0]):
                o_ref[b, :, s * LANES:(s + 1) * LANES] = slabs[s, pl.ds(b, rows // BATCH, stride=BATCH), :]
    else:
        o_ref[...] = x2


def _post(x, hf, ab, bb, pieces, mod, par, w_out_b, w1_b, w2_b, h0, *, layer, is_ctx, final_norm=False,
          natural_out=False, casts=()):
    natural_in = x.ndim == 3
    n_rows = x.shape[0] * x.shape[1] if natural_in else x.shape[0]
    n_tiles = n_rows // TILE_ROWS
    gt, v, bg = pieces
    scratch = [pltpu.VMEM((BATCH, D_LRU), F32)]
    if natural_in or natural_out:
        scratch.append(pltpu.VMEM((D_MODEL // LANES, TILE_ROWS, LANES), F32))
    if natural_in:
        x_in_spec = pl.BlockSpec((BATCH, TILE_STEPS, D_MODEL), lambda i: (0, n_tiles - 1 - i, 0))
    else:
        x_in_spec = pl.BlockSpec((TILE_ROWS, D_MODEL), lambda i: (n_tiles - 1 - i, 0))
    if natural_out:
        x_out_spec = pl.BlockSpec((BATCH, TILE_STEPS, D_MODEL), lambda i: (0, n_tiles - 1 - i, 0))
        x_out_shape = jax.ShapeDtypeStruct((BATCH, n_rows // BATCH, D_MODEL), F32)
    else:
        x_out_spec = pl.BlockSpec((TILE_ROWS, D_MODEL), lambda i: (n_tiles - 1 - i, 0))
        x_out_shape = jax.ShapeDtypeStruct((n_rows, D_MODEL), F32)

    def rev(i):
        return n_tiles - 1 - i

    tile = lambda width: pl.BlockSpec((TILE_ROWS, width), lambda i: (rev(i), 0))
    if is_ctx:
        per = TILE_ROWS // BATCH
        last = n_rows // BATCH - 1
        prev_spec = pl.BlockSpec((BATCH, D_CONV), lambda i: (jnp.maximum(rev(i) * per - 1, 0), 0))
        next_spec = pl.BlockSpec((BATCH, D_CONV), lambda i: (jnp.minimum((rev(i) + 1) * per, last), 0))
    else:
        prev_spec = pl.BlockSpec((TILE_ROWS, HALF_CONV), lambda i: (jnp.maximum(rev(i) - 1, 0), 1))
        next_spec = pl.BlockSpec((TILE_ROWS, HALF_CONV), lambda i: (jnp.minimum(rev(i) + 1, n_tiles - 1), 1))
    args = [x, hf, ab, bb, gt, v, bg, v, v, mod, par, w_out_b, w1_b, w2_b, h0]
    in_specs = ([x_in_spec] + [tile(D_LRU)] * 6 + [prev_spec, next_spec]
                + [_layer_spec(mod.shape, layer), _layer_spec(par.shape, layer), _const_spec(w_out_b.shape),
                   _const_spec(w1_b.shape), _const_spec(w2_b.shape), _const_spec((BATCH, D_LRU))])
    cast_in, cast_out, cast_shapes, cast_ops = _cast_specs(casts, n_tiles)
    return pl.pallas_call(
        functools.partial(_post_kernel, is_ctx=is_ctx, final_norm=final_norm, natural_in=natural_in,
                          natural_out=natural_out, n_casts=len(casts)),
        grid=(n_tiles,),
        in_specs=in_specs + cast_in,
        out_specs=[x_out_spec, pl.BlockSpec((BATCH, D_LRU), lambda i: (0, 0))] + cast_out,
        out_shape=[x_out_shape, jax.ShapeDtypeStruct((BATCH, D_LRU), F32)] + cast_shapes,
        scratch_shapes=scratch,
        compiler_params=pltpu.CompilerParams(
            dimension_semantics=("arbitrary",), vmem_limit_bytes=VMEM_LIMIT_BYTES),
        name="post",
    )(*args, *cast_ops)


def _pack_gates(w_a, w_x):
    per = GATE_BLOCK // LRU_HEAD_DIM
    n_blk = LRU_HEADS // per
    eye = jnp.eye(per, dtype=F32)

    def blocks(w):
        w = w.reshape(w.shape[0], N_DIR, n_blk, per, LRU_HEAD_DIM, LRU_HEAD_DIM)
        return jnp.einsum('ldkhij,hg->ldkhigj', w, eye).reshape(w.shape[0], N_DIR, n_blk, GATE_BLOCK, GATE_BLOCK)

    return (0.5 * jnp.concatenate([blocks(w_a), blocks(w_x)], axis=-1).astype(BF16)).astype(BF16)


def _pack_small(norm1_g, norm2_g, final_g, conv4_w, conv4_b, g_out_lru, g_out_conv, conv3_w, gate_a_b, gate_x_b,
                rg_lambda):
    depth = norm1_g.shape[0]
    pair = lambda a, b: jnp.concatenate([a, b], axis=-1)
    rows = [None] * (ROW_LAMBDA + 1)
    rows[ROW_NORM1] = norm1_g
    rows[ROW_NORM2] = norm2_g
    rows[ROW_FINAL] = jnp.broadcast_to(final_g[None, :], (depth, D_MODEL))
    rows[ROW_C4W01] = pair(conv4_w[:, 0], conv4_w[:, 1])
    rows[ROW_C4W23] = pair(conv4_w[:, 2], conv4_w[:, 3])
    rows[ROW_C4B_GLRU] = pair(conv4_b, g_out_lru)
    rows[ROW_GCONV_C3W0] = pair(g_out_conv, conv3_w[:, 0])
    rows[ROW_C3W12] = pair(conv3_w[:, 1], conv3_w[:, 2])
    for d in range(N_DIR):
        rows[ROW_GATE_B + d] = 0.5 * pair(gate_a_b[:, d], gate_x_b[:, d])
    rows[ROW_LAMBDA] = pair(rg_lambda[:, 0], rg_lambda[:, 1])
    rows += [jnp.zeros((depth, D_MODEL), F32)] * (PAR_ROWS - len(rows))
    return jnp.stack(rows, axis=1)


def kernel(x, c, ctx, c_ctx, ada_w, ada_b, norm1_g, norm2_g, w_in, conv4_w, conv4_b, gate_a_w, gate_a_b,
           gate_x_w, gate_x_b, rg_lambda, conv3_w, g_out_lru, g_out_conv, w_out, w_mlp1, w_mlp2, final_g):
    depth = ada_w.shape[0]
    bsz, seq, _ = x.shape
    assert bsz == BATCH and seq % TILE_STEPS == 0 and ctx.shape[1] % TILE_STEPS == 0
    cc = jnp.concatenate([c, c_ctx[None, :], jnp.zeros((MOD_ROWS - bsz - 1, D_MODEL), F32)], axis=0)
    mod, w_in_b = _ada(cc, ada_w, ada_b, w_in)
    par = _pack_small(norm1_g, norm2_g, final_g, conv4_w, conv4_b, g_out_lru, g_out_conv, conv3_w, gate_a_b,
                      gate_x_b, rg_lambda)
    wg = _pack_gates(gate_a_w, gate_x_w)
    zero_state = jnp.zeros((BATCH, D_LRU), F32)

    for l in range(depth):
        last = l == depth - 1
        cx = _proj(ctx, mod, par, w_in_b, wg, zero_state, layer=l, is_ctx=True, n_pieces=0 if last else N_PIECES)
        lat = _proj(x, mod, par, w_in_b, wg, cx[3], layer=l, is_ctx=False, n_pieces=N_PIECES,
                    casts=[(w_out, l), (w_mlp1, l), (w_mlp2, l)])
        w_out_b, w1_b, w2_b = lat[4 + N_PIECES:]
        if last:
            h0_bwd = _ctx_state(cx[1], cx[2])
        else:
            ctx, h0_bwd = _post(ctx, cx[0], cx[1], cx[2], cx[4:], mod, par, w_out_b, w1_b, w2_b, zero_state,
                                layer=l, is_ctx=True)
        res = _post(x, lat[0], lat[1], lat[2], lat[4:4 + N_PIECES], mod, par, w_out_b, w1_b, w2_b, h0_bwd,
                    layer=l, is_ctx=False, final_norm=last, natural_out=last,
                    casts=[] if last else [(w_in, l + 1)])
        x = res[0]
        if not last:
            w_in_b = res[2]
    return x
```

```python
import functools

import jax
import jax.numpy as jnp
from jax import lax
from jax.experimental import pallas as pl
from jax.experimental.pallas import tpu as pltpu

F32 = jnp.float32
BF16 = jnp.bfloat16

D_MODEL = 1024
D_LRU = 512
D_CONV = 512
HALF_CONV = D_CONV // 2
D_FF = 4 * D_MODEL
GRID_W = 64
LRU_HEADS = 8
LRU_HEAD_DIM = D_LRU // LRU_HEADS
N_MOD = 6
SH1, SC1, G1, SH2, SC2, G2 = range(N_MOD)
COL_LRU, COL_GATE, COL_X, COL_B, COL_C = range(5)
ST_HF, ST_AB, ST_BB, ST_GT, ST_V, ST_BG = range(6)
N_STREAMS = 6
N_STREAMS_LRU = 3
N_DIR = 2
RG_C = 8.0
EPS = 1e-6
LOG2_E = 1.4426950408889634

SUBLANES = 8
LANES = 128
BF16_SUBLANES = 2 * SUBLANES
BATCH = SUBLANES
GATE_BLOCK = 256
MOD_ROWS = 16
VMEM_LIMIT_BYTES = 56 * 1024 * 1024

ADA_COLS = 1536
TILE_STEPS = GRID_W
TILE_ROWS = TILE_STEPS * BATCH
PROJ_STEPS = TILE_STEPS
PROJ_ROWS = PROJ_STEPS * BATCH
FF_CHUNK = 1024

PAR_ROWS = 16
ROW_NORM1, ROW_NORM2, ROW_FINAL, ROW_C4W01, ROW_C4W23, ROW_C4B_GLRU, ROW_GCONV_C3W0, ROW_C3W12 = range(8)
ROW_GATE_B = 8
ROW_LAMBDA = 10


def _sigmoid(z):
    return 0.5 * (jnp.tanh(0.5 * z) + 1.0)


def _gelu_tanh(z):
    return 0.5 * z * (1.0 + jnp.tanh(0.7978845608028654 * (z + 0.044715 * (z * z * z))))


def _unit_rms(x):
    return x * lax.rsqrt(jnp.mean(x * x, axis=-1, keepdims=True) + EPS)


def _rms(x, g):
    return _unit_rms(x) * g


def _dot(a, b):
    return jnp.dot(a, b, preferred_element_type=F32)


def _per_batch(x, fn):
    rows, d = x.shape
    return fn(x.reshape(rows // BATCH, BATCH, d)).reshape(rows, d)


def _norm_modulate(x, g, shift, scale):
    gain = g * (1.0 + scale)
    return _per_batch(_unit_rms(x), lambda y3: y3 * gain[None] + shift[None])


def _half(row, k):
    return row[:, k * D_LRU:(k + 1) * D_LRU]


def _mod_rows(mod_ref, k, is_ctx):
    cols = slice(k * D_MODEL, (k + 1) * D_MODEL)
    if is_ctx:
        return jnp.broadcast_to(mod_ref[0, BATCH:BATCH + 1, cols], (BATCH, D_MODEL))
    return mod_ref[0, 0:BATCH, cols]


def _const_spec(shape):
    nd = len(shape)
    return pl.BlockSpec(shape, lambda *_: (0,) * nd, pipeline_mode=pl.Buffered(1))


def _layer_spec(shape, layer):
    nd = len(shape)
    return pl.BlockSpec((1,) + tuple(shape[1:]), lambda *_: (layer,) + (0,) * (nd - 1), pipeline_mode=pl.Buffered(1))


def _ada_kernel(cc_ref, w_ref, b_ref, win_ref, o_ref, winb_ref):
    cc = cc_ref[...]
    s = (cc * _sigmoid(cc)).astype(BF16)
    o_ref[0] = _dot(s, w_ref[0].astype(BF16)) + b_ref[0]
    winb_ref[...] = win_ref[0].astype(BF16)


def _ada(cc, ada_w, ada_b, w_in):
    depth, _, n = ada_w.shape
    n_j = n // ADA_COLS
    rows, cols = w_in.shape[1:]
    chunk = rows // (depth * n_j)
    assert chunk * depth * n_j == rows and chunk % BF16_SUBLANES == 0
    return pl.pallas_call(
        _ada_kernel,
        grid=(depth, n_j),
        in_specs=[
            pl.BlockSpec((MOD_ROWS, D_MODEL), lambda l, j: (0, 0)),
            pl.BlockSpec((1, D_MODEL, ADA_COLS), lambda l, j: (l, 0, j)),
            pl.BlockSpec((1, 1, ADA_COLS), lambda l, j: (l, 0, j)),
            pl.BlockSpec((1, chunk, cols), lambda l, j: (0, l * n_j + j, 0)),
        ],
        out_specs=[pl.BlockSpec((1, MOD_ROWS, ADA_COLS), lambda l, j: (l, 0, j)),
                   pl.BlockSpec((chunk, cols), lambda l, j: (l * n_j + j, 0))],
        out_shape=[jax.ShapeDtypeStruct((depth, MOD_ROWS, n), F32), jax.ShapeDtypeStruct((rows, cols), BF16)],
        compiler_params=pltpu.CompilerParams(
            dimension_semantics=("arbitrary", "arbitrary"), vmem_limit_bytes=VMEM_LIMIT_BYTES),
        name="ada",
    )(cc, ada_w, ada_b.reshape(depth, 1, n), w_in)


def _half_rate(lam):
    z = -lam
    return (-0.5 * RG_C * LOG2_E) * (jnp.maximum(z, 0.0) + jnp.log(1.0 + jnp.exp(-jnp.abs(z))))


def _lru_gates(vb, w_lo, w_hi, bias_a, bias_x):
    g_lo = _dot(vb[:, :GATE_BLOCK], w_lo)
    g_hi = _dot(vb[:, GATE_BLOCK:], w_hi)
    th_a = jnp.tanh(jnp.concatenate([g_lo[:, :GATE_BLOCK], g_hi[:, :GATE_BLOCK]], axis=1) + bias_a)
    th_x = jnp.tanh(jnp.concatenate([g_lo[:, GATE_BLOCK:], g_hi[:, GATE_BLOCK:]], axis=1) + bias_x)
    return th_a, th_x


def _lru_coeffs(half_v, th_a, th_x, half_rate):
    a = jnp.exp2(half_rate * th_a + half_rate)
    s = 1.0 - a * a
    root = jnp.where(s > 0.0, s * lax.rsqrt(s), 0.0)
    b = root * (half_v * th_x + half_v)
    return a, b


def _recurrence(a, b, h, reverse):
    steps = a.shape[0] // BATCH
    out = [None] * steps
    for t in (reversed(range(steps)) if reverse else range(steps)):
        sl = slice(t * BATCH, (t + 1) * BATCH)
        h = a[sl] * h + b[sl]
        out[t] = h
    return jnp.concatenate(out, axis=0), h


def _to_time_major(x_ref, xs):
    steps = x_ref.shape[1]
    for b in range(BATCH):
        for s in range(xs.shape[0]):
            xs[s, pl.ds(b, steps, stride=BATCH), :] = x_ref[b, :, s * LANES:(s + 1) * LANES]
    return jnp.concatenate([xs[s] for s in range(xs.shape[0])], axis=1)


def _cast_chunks(in_refs, out_refs):
    for w_ref, o_ref in zip(in_refs, out_refs):
        o_ref[...] = w_ref[0].astype(BF16)


def _cast_specs(casts, n_steps):
    in_specs, out_specs, out_shapes, operands = [], [], [], []
    for w, layer in casts:
        _, rows, cols = w.shape
        chunk = rows // n_steps
        assert chunk * n_steps == rows and chunk % BF16_SUBLANES == 0
        in_specs.append(pl.BlockSpec((1, chunk, cols), lambda i, layer=layer: (layer, i, 0)))
        out_specs.append(pl.BlockSpec((chunk, cols), lambda i: (i, 0)))
        out_shapes.append(jax.ShapeDtypeStruct((rows, cols), BF16))
        operands.append(w)
    return in_specs, out_specs, out_shapes, operands


def _proj_kernel(x_ref, xp_ref, xn_ref, mod_ref, par_ref, w_ref, wg_ref, h0_ref, *rest, natural_in, is_ctx,
                 n_streams, n_casts):
    rest = list(rest)
    cast_in, rest = rest[:n_casts], rest[n_casts:]
    st_ref, hend_ref = rest[:2]
    cast_out, rest = rest[2:2 + n_casts], rest[2 + n_casts:]
    carry_s = rest[0]
    xs = rest[1] if natural_in else None
    _cast_chunks(cast_in, cast_out)
    j = pl.program_id(0)
    n_j = pl.num_programs(0)
    rows = PROJ_ROWS
    par = par_ref[0]
    if natural_in:
        last = xp_ref.shape[1] - 1
        xe = jnp.concatenate([xp_ref[:, last, :], _to_time_major(x_ref, xs), xn_ref[:, 0, :], xn_ref[:, 1, :]],
                             axis=0)
    else:
        xe = jnp.concatenate([xp_ref[...], x_ref[...], xn_ref[...]], axis=0)
    hb = _norm_modulate(xe, par[ROW_NORM1:ROW_NORM1 + 1], _mod_rows(mod_ref, SH1, is_ctx),
                        _mod_rows(mod_ref, SC1, is_ctx)).astype(BF16)
    hb_c = hb[BATCH:BATCH + rows]

    def group(k):
        return _dot(hb_c, w_ref[:, k * D_LRU:(k + 1) * D_LRU])

    def emit(k, value):
        st_ref[:, k * D_LRU:(k + 1) * D_LRU] = value

    full = n_streams == N_STREAMS

    u_ext = _dot(hb, w_ref[:, :D_LRU])
    u_ext = jnp.concatenate([jnp.where(j > 0, u_ext[:BATCH], 0.0), u_ext[BATCH:BATCH + rows],
                             jnp.where(j < n_j - 1, u_ext[BATCH + rows:], 0.0)], axis=0)
    if full:
        emit(ST_GT, group(COL_GATE))
        emit(ST_BG, group(COL_B))
    v = _half(par[ROW_C4B_GLRU:ROW_C4B_GLRU + 1], 0)
    for k in range(4):
        tap = _half(par[ROW_C4W01 + k // 2:ROW_C4W01 + k // 2 + 1], k % 2)
        v = v + tap * u_ext[k * BATCH:k * BATCH + rows]
    vb = v.astype(BF16)
    half_v = 0.5 * v
    gates = []
    for d in range(N_DIR):
        bias = par[ROW_GATE_B + d:ROW_GATE_B + d + 1]
        gates.append(_lru_gates(vb, wg_ref[0, d, 0], wg_ref[0, d, 1], _half(bias, 0), _half(bias, 1)))
    if full:
        emit(ST_V, group(COL_C) * group(COL_X))
    lam = par[ROW_LAMBDA:ROW_LAMBDA + 1]
    a_f, b_f = _lru_coeffs(half_v, gates[0][0], gates[0][1], _half_rate(_half(lam, 0)))
    a_b, b_b = _lru_coeffs(half_v, gates[1][0], gates[1][1], _half_rate(_half(lam, 1)))
    emit(ST_AB, a_b)
    emit(ST_BB, b_b)

    @pl.when(j == 0)
    def _():
        carry_s[...] = h0_ref[...]

    h_f, h_last = _recurrence(a_f, b_f, carry_s[...], False)
    emit(ST_HF, h_f)
    carry_s[...] = h_last
    hend_ref[...] = h_last


def _proj(x, mod, par, w_in_b, wg, h0, *, layer, is_ctx, n_streams, casts=()):
    natural_in = x.ndim == 3
    n_rows = x.shape[0] * x.shape[1] if natural_in else x.shape[0]
    n_tiles = n_rows // PROJ_ROWS
    n_cols = (COL_C + 1) * D_LRU if n_streams == N_STREAMS else D_LRU
    tile = pl.BlockSpec((PROJ_ROWS, n_streams * D_LRU), lambda j: (j, 0))
    stream = jax.ShapeDtypeStruct((n_rows, n_streams * D_LRU), F32)
    state = jax.ShapeDtypeStruct((BATCH, D_LRU), F32)
    scratch = [pltpu.VMEM((BATCH, D_LRU), F32)]
    if natural_in:
        halo = SUBLANES
        per = PROJ_STEPS // halo
        last_halo = x.shape[1] // halo - 1
        x_specs = [
            pl.BlockSpec((BATCH, PROJ_STEPS, D_MODEL), lambda j: (0, j, 0)),
            pl.BlockSpec((BATCH, halo, D_MODEL), lambda j: (0, jnp.maximum(j * per - 1, 0), 0)),
            pl.BlockSpec((BATCH, halo, D_MODEL), lambda j: (0, jnp.minimum((j + 1) * per, last_halo), 0)),
        ]
        scratch.append(pltpu.VMEM((D_MODEL // LANES, PROJ_ROWS, LANES), F32))
    else:
        per_prev = PROJ_ROWS // BATCH
        per_next = PROJ_ROWS // (2 * BATCH)
        last_next = n_rows // (2 * BATCH) - 1
        x_specs = [
            pl.BlockSpec((PROJ_ROWS, D_MODEL), lambda j: (j, 0)),
            pl.BlockSpec((BATCH, D_MODEL), lambda j: (jnp.maximum(j * per_prev - 1, 0), 0)),
            pl.BlockSpec((2 * BATCH, D_MODEL), lambda j: (jnp.minimum((j + 1) * per_next, last_next), 0)),
        ]
    cast_in, cast_out, cast_shapes, cast_ops = _cast_specs(casts, n_tiles)
    return pl.pallas_call(
        functools.partial(_proj_kernel, natural_in=natural_in, is_ctx=is_ctx, n_streams=n_streams,
                          n_casts=len(casts)),
        grid=(n_tiles,),
        in_specs=x_specs + [
            _layer_spec(mod.shape, layer),
            _layer_spec(par.shape, layer),
            _const_spec((D_MODEL, n_cols)),
            _layer_spec(wg.shape, layer),
            _const_spec((BATCH, D_LRU)),
        ] + cast_in,
        out_specs=[tile, pl.BlockSpec((BATCH, D_LRU), lambda j: (0, 0))] + cast_out,
        out_shape=[stream, state] + cast_shapes,
        scratch_shapes=scratch,
        compiler_params=pltpu.CompilerParams(
            dimension_semantics=("arbitrary",), vmem_limit_bytes=VMEM_LIMIT_BYTES),
        name="proj",
    )(x, x, x, mod, par, w_in_b, wg, h0, *cast_ops)


def _ctx_state_kernel(a_ref, b_ref, o_ref):
    _, o_ref[...] = _recurrence(a_ref[...], b_ref[...], jnp.zeros((BATCH, D_LRU), F32), True)


def _ctx_state(st):
    n_rows = st.shape[0]
    return pl.pallas_call(
        _ctx_state_kernel,
        grid=(1,),
        in_specs=[pl.BlockSpec((n_rows, D_LRU), lambda i: (0, ST_AB)),
                  pl.BlockSpec((n_rows, D_LRU), lambda i: (0, ST_BB))],
        out_specs=pl.BlockSpec((BATCH, D_LRU), lambda i: (0, 0)),
        out_shape=jax.ShapeDtypeStruct((BATCH, D_LRU), F32),
        compiler_params=pltpu.CompilerParams(
            dimension_semantics=("arbitrary",), vmem_limit_bytes=VMEM_LIMIT_BYTES),
        name="ctx_state",
    )(st, st)


def _post_kernel(x_ref, st_ref, vp_ref, vn_ref, mod_ref, par_ref, wo_ref, w1_ref, w2_ref, h0_ref, *rest, is_ctx,
                 final_norm, natural_in, natural_out, n_casts):
    rest = list(rest)
    cast_in, rest = rest[:n_casts], rest[n_casts:]
    o_ref, hend_ref = rest[:2]
    cast_out, rest = rest[2:2 + n_casts], rest[2 + n_casts:]
    carry_s = rest[0]
    slabs = rest[1] if (natural_in or natural_out) else None
    _cast_chunks(cast_in, cast_out)

    i = pl.program_id(0)
    n_i = pl.num_programs(0)
    j = n_i - 1 - i
    rows = TILE_ROWS
    par = par_ref[0]
    g_lru = _half(par[ROW_C4B_GLRU:ROW_C4B_GLRU + 1], 1)
    g_conv = _half(par[ROW_GCONV_C3W0:ROW_GCONV_C3W0 + 1], 0)
    w3 = [_half(par[ROW_GCONV_C3W0:ROW_GCONV_C3W0 + 1], 1), _half(par[ROW_C3W12:ROW_C3W12 + 1], 0),
          _half(par[ROW_C3W12:ROW_C3W12 + 1], 1)]

    @pl.when(i == 0)
    def _():
        carry_s[...] = h0_ref[...]

    stream = lambda k: st_ref[:, k * D_LRU:(k + 1) * D_LRU]
    h_b, h_last = _recurrence(stream(ST_AB), stream(ST_BB), carry_s[...], True)
    carry_s[...] = h_last
    hend_ref[...] = h_last
    y_lru = _gelu_tanh(stream(ST_GT)) * (stream(ST_HF) + h_b)

    v = stream(ST_V)
    zeros = jnp.zeros((BATCH, D_CONV), F32)
    if is_ctx:
        before = jnp.where(j > 0, vp_ref[...], 0.0)
        after = jnp.where(j < n_i - 1, vn_ref[...], 0.0)
        v_m = jnp.concatenate([before, v[:rows - BATCH]], axis=0)
        v_p = jnp.concatenate([v[BATCH:], after], axis=0)
        conv = w3[0] * v_m + w3[1] * v + w3[2] * v_p
    else:
        va = v[:, :HALF_CONV]
        va_m = jnp.concatenate([zeros[:, :HALF_CONV], va[:rows - BATCH]], axis=0)
        va_p = jnp.concatenate([va[BATCH:], zeros[:, :HALF_CONV]], axis=0)
        conv_row = w3[0][:, :HALF_CONV] * va_m + w3[1][:, :HALF_CONV] * va + w3[2][:, :HALF_CONV] * va_p
        vb_m = jnp.where(j > 0, vp_ref[...], 0.0)
        vb_p = jnp.where(j < n_i - 1, vn_ref[...], 0.0)
        conv_col = (w3[0][:, HALF_CONV:] * vb_m + w3[1][:, HALF_CONV:] * v[:, HALF_CONV:]
                    + w3[2][:, HALF_CONV:] * vb_p)
        conv = jnp.concatenate([conv_row, conv_col], axis=1)
    y_conv = stream(ST_BG) * conv

    yl = _rms(y_lru, g_lru).astype(BF16)
    yc = _rms(y_conv, g_conv).astype(BF16)
    mix = _dot(yl, wo_ref[0:D_LRU, :]) + _dot(yc, wo_ref[D_LRU:, :])
    x_in = _to_time_major(x_ref, slabs) if natural_in else x_ref[...]
    gate1 = _mod_rows(mod_ref, G1, is_ctx)
    x1 = x_in + _per_batch(mix, lambda t3: t3 * gate1[None])

    hb = _norm_modulate(x1, par[ROW_NORM2:ROW_NORM2 + 1], _mod_rows(mod_ref, SH2, is_ctx),
                        _mod_rows(mod_ref, SC2, is_ctx)).astype(BF16)
    acc = jnp.zeros((rows, D_MODEL), F32)
    for k in range(D_FF // FF_CHUNK):
        t1 = jnp.maximum(_dot(hb, w1_ref[:, k * FF_CHUNK:(k + 1) * FF_CHUNK]), 0.0)
        acc = acc + _dot((t1 * t1).astype(BF16), w2_ref[k * FF_CHUNK:(k + 1) * FF_CHUNK, :])
    gate2 = _mod_rows(mod_ref, G2, is_ctx)
    x2 = x1 + _per_batch(acc, lambda t3: t3 * gate2[None])
    if final_norm:
        x2 = _rms(x2, par[ROW_FINAL:ROW_FINAL + 1])
    if natural_out:
        for s in range(slabs.shape[0]):
            slabs[s] = x2[:, s * LANES:(s + 1) * LANES]
        for b in range(BATCH):
            for s in range(slabs.shape[0]):
                o_ref[b, :, s * LANES:(s + 1) * LANES] = slabs[s, pl.ds(b, rows // BATCH, stride=BATCH), :]
    else:
        o_ref[...] = x2


def _post(x, st, mod, par, w_out_b, w1_b, w2_b, h0, *, layer, is_ctx, final_norm=False, natural_out=False,
          casts=()):
    natural_in = x.ndim == 3
    n_rows = x.shape[0] * x.shape[1] if natural_in else x.shape[0]
    n_tiles = n_rows // TILE_ROWS
    scratch = [pltpu.VMEM((BATCH, D_LRU), F32)]
    if natural_in or natural_out:
        scratch.append(pltpu.VMEM((D_MODEL // LANES, TILE_ROWS, LANES), F32))
    if natural_in:
        x_in_spec = pl.BlockSpec((BATCH, TILE_STEPS, D_MODEL), lambda i: (0, n_tiles - 1 - i, 0))
    else:
        x_in_spec = pl.BlockSpec((TILE_ROWS, D_MODEL), lambda i: (n_tiles - 1 - i, 0))
    if natural_out:
        x_out_spec = pl.BlockSpec((BATCH, TILE_STEPS, D_MODEL), lambda i: (0, n_tiles - 1 - i, 0))
        x_out_shape = jax.ShapeDtypeStruct((BATCH, n_rows // BATCH, D_MODEL), F32)
    else:
        x_out_spec = pl.BlockSpec((TILE_ROWS, D_MODEL), lambda i: (n_tiles - 1 - i, 0))
        x_out_shape = jax.ShapeDtypeStruct((n_rows, D_MODEL), F32)

    def rev(i):
        return n_tiles - 1 - i

    st_spec = pl.BlockSpec((TILE_ROWS, N_STREAMS * D_LRU), lambda i: (rev(i), 0))
    if is_ctx:
        per = TILE_ROWS // BATCH
        last = n_rows // BATCH - 1
        prev_spec = pl.BlockSpec((BATCH, D_CONV), lambda i: (jnp.maximum(rev(i) * per - 1, 0), ST_V))
        next_spec = pl.BlockSpec((BATCH, D_CONV), lambda i: (jnp.minimum((rev(i) + 1) * per, last), ST_V))
    else:
        col_half = ST_V * (D_CONV // HALF_CONV) + 1
        prev_spec = pl.BlockSpec((TILE_ROWS, HALF_CONV), lambda i: (jnp.maximum(rev(i) - 1, 0), col_half))
        next_spec = pl.BlockSpec((TILE_ROWS, HALF_CONV), lambda i: (jnp.minimum(rev(i) + 1, n_tiles - 1), col_half))
    args = [x, st, st, st, mod, par, w_out_b, w1_b, w2_b, h0]
    in_specs = ([x_in_spec, st_spec, prev_spec, next_spec]
                + [_layer_spec(mod.shape, layer), _layer_spec(par.shape, layer), _const_spec(w_out_b.shape),
                   _const_spec(w1_b.shape), _const_spec(w2_b.shape), _const_spec((BATCH, D_LRU))])
    cast_in, cast_out, cast_shapes, cast_ops = _cast_specs(casts, n_tiles)
    return pl.pallas_call(
        functools.partial(_post_kernel, is_ctx=is_ctx, final_norm=final_norm, natural_in=natural_in,
                          natural_out=natural_out, n_casts=len(casts)),
        grid=(n_tiles,),
        in_specs=in_specs + cast_in,
        out_specs=[x_out_spec, pl.BlockSpec((BATCH, D_LRU), lambda i: (0, 0))] + cast_out,
        out_shape=[x_out_shape, jax.ShapeDtypeStruct((BATCH, D_LRU), F32)] + cast_shapes,
        scratch_shapes=scratch,
        compiler_params=pltpu.CompilerParams(
            dimension_semantics=("arbitrary",), vmem_limit_bytes=VMEM_LIMIT_BYTES),
        name="post",
    )(*args, *cast_ops)


def _pack_gates(w_a, w_x):
    per = GATE_BLOCK // LRU_HEAD_DIM
    n_blk = LRU_HEADS // per
    eye = jnp.eye(per, dtype=F32)

    def blocks(w):
        w = w.reshape(w.shape[0], N_DIR, n_blk, per, LRU_HEAD_DIM, LRU_HEAD_DIM)
        return jnp.einsum('ldkhij,hg->ldkhigj', w, eye).reshape(w.shape[0], N_DIR, n_blk, GATE_BLOCK, GATE_BLOCK)

    return (0.5 * jnp.concatenate([blocks(w_a), blocks(w_x)], axis=-1).astype(BF16)).astype(BF16)


def _pack_small(norm1_g, norm2_g, final_g, conv4_w, conv4_b, g_out_lru, g_out_conv, conv3_w, gate_a_b, gate_x_b,
                rg_lambda):
    depth = norm1_g.shape[0]
    pair = lambda a, b: jnp.concatenate([a, b], axis=-1)
    rows = [None] * (ROW_LAMBDA + 1)
    rows[ROW_NORM1] = norm1_g
    rows[ROW_NORM2] = norm2_g
    rows[ROW_FINAL] = jnp.broadcast_to(final_g[None, :], (depth, D_MODEL))
    rows[ROW_C4W01] = pair(conv4_w[:, 0], conv4_w[:, 1])
    rows[ROW_C4W23] = pair(conv4_w[:, 2], conv4_w[:, 3])
    rows[ROW_C4B_GLRU] = pair(conv4_b, g_out_lru)
    rows[ROW_GCONV_C3W0] = pair(g_out_conv, conv3_w[:, 0])
    rows[ROW_C3W12] = pair(conv3_w[:, 1], conv3_w[:, 2])
    for d in range(N_DIR):
        rows[ROW_GATE_B + d] = 0.5 * pair(gate_a_b[:, d], gate_x_b[:, d])
    rows[ROW_LAMBDA] = pair(rg_lambda[:, 0], rg_lambda[:, 1])
    rows += [jnp.zeros((depth, D_MODEL), F32)] * (PAR_ROWS - len(rows))
    return jnp.stack(rows, axis=1)


def kernel(x, c, ctx, c_ctx, ada_w, ada_b, norm1_g, norm2_g, w_in, conv4_w, conv4_b, gate_a_w, gate_a_b,
           gate_x_w, gate_x_b, rg_lambda, conv3_w, g_out_lru, g_out_conv, w_out, w_mlp1, w_mlp2, final_g):
    depth = ada_w.shape[0]
    bsz, seq, _ = x.shape
    assert bsz == BATCH and seq % PROJ_STEPS == 0 and ctx.shape[1] % PROJ_STEPS == 0
    cc = jnp.concatenate([c, c_ctx[None, :], jnp.zeros((MOD_ROWS - bsz - 1, D_MODEL), F32)], axis=0)
    mod, w_in_b = _ada(cc, ada_w, ada_b, w_in)
    par = _pack_small(norm1_g, norm2_g, final_g, conv4_w, conv4_b, g_out_lru, g_out_conv, conv3_w, gate_a_b,
                      gate_x_b, rg_lambda)
    wg = _pack_gates(gate_a_w, gate_x_w)
    zero_state = jnp.zeros((BATCH, D_LRU), F32)

    for l in range(depth):
        last = l == depth - 1
        cx_st, cx_end = _proj(ctx, mod, par, w_in_b, wg, zero_state, layer=l, is_ctx=True,
                              n_streams=N_STREAMS_LRU if last else N_STREAMS)
        lat_st, _, w_out_b, w1_b, w2_b = _proj(x, mod, par, w_in_b, wg, cx_end, layer=l, is_ctx=False,
                                               n_streams=N_STREAMS,
                                               casts=[(w_out, l), (w_mlp1, l), (w_mlp2, l)])
        if last:
            h0_bwd = _ctx_state(cx_st)
        else:
            ctx, h0_bwd = _post(ctx, cx_st, mod, par, w_out_b, w1_b, w2_b, zero_state, layer=l, is_ctx=True)
        res = _post(x, lat_st, mod, par, w_out_b, w1_b, w2_b, h0_bwd, layer=l, is_ctx=False, final_norm=last,
                    natural_out=last, casts=[] if last else [(w_in, l + 1)])
        x = res[0]
        if not last:
            w_in_b = res[2]
    return x
```
